```python
import math
import jax, jax.numpy as jnp
from jax import lax
import numpy as np

D_MODEL = 1024
BATCH = 8
SEQ = 8192
DEPTH = 2

N_HEADS = 16
HEAD_DIM = 64
N_KV_HEADS = 2
Q_GROUP = N_HEADS // N_KV_HEADS
ATTN_WIDTH = N_HEADS * HEAD_DIM
KV_WIDTH = N_KV_HEADS * HEAD_DIM
WINDOW = 128
BLOCK = 128
N_META = 16
META_PAD = BLOCK - N_META
POOL_WINDOWS = (2, 4, 8, 16)
N_POOL_GROUPS = 4
POOL_GROUP_DIM = 128
POOL_WIDTH = N_POOL_GROUPS * POOL_GROUP_DIM
POOL_GROUP_OUT = D_MODEL // N_POOL_GROUPS
N_BRANCHES = 2
IN_WIDTH = ATTN_WIDTH + 2 * KV_WIDTH + POOL_WIDTH + N_BRANCHES * D_MODEL
D_FF = 2816
N_EXPERTS = 8
TOP_K = 2
D_FF_EXPERT = 3584
N_DENSE = (DEPTH + 1) // 2
N_MOE = DEPTH // 2
RMS_EPS = 1e-5
NEG_INF = -1e30

kernel_name = "hybrid_swa_sink_pool_gated_moe"


def rms_norm(x, gain):
    xf = x.astype(jnp.float32)
    y = xf * lax.rsqrt(jnp.mean(xf * xf, axis=-1, keepdims=True) + RMS_EPS)
    return (y * gain.astype(jnp.float32)).astype(x.dtype)


def alibi_slopes():
    return jnp.asarray(np.array([2.0 ** (-8.0 * (h + 1) / N_HEADS) for h in range(N_HEADS)], dtype=np.float32))


def swa_sink_attention(q, k, v, sinks, slopes):
    B, L = q.shape[0], q.shape[1]
    P = L + META_PAD
    nblk = P // BLOCK

    def pad_front(t, n):
        return jnp.pad(t, ((0, 0), (n, 0), (0, 0), (0, 0)))

    qb = pad_front(q, META_PAD).reshape(B, nblk, BLOCK, N_KV_HEADS, Q_GROUP, HEAD_DIM)

    def band(t):
        tp = pad_front(t, META_PAD)
        prev = pad_front(tp, BLOCK)[:, :P]
        return jnp.concatenate([prev.reshape(B, nblk, BLOCK, N_KV_HEADS, HEAD_DIM),
                                tp.reshape(B, nblk, BLOCK, N_KV_HEADS, HEAD_DIM)], axis=2)

    kb, vb = band(k), band(v)
    km, vm = k[:, :N_META], v[:, :N_META]
    scale = HEAD_DIM ** -0.5
    s_band = jnp.einsum('bnqkgd,bnskd->bkgnqs', qb, kb, preferred_element_type=jnp.float32) * scale
    s_meta = jnp.einsum('bnqkgd,bmkd->bkgnqm', qb, km, preferred_element_type=jnp.float32) * scale

    blk = jnp.arange(nblk)[:, None] * BLOCK
    t_q = blk + jnp.arange(BLOCK)[None, :]
    t_band = blk - BLOCK + jnp.arange(2 * BLOCK)[None, :]
    t_meta = META_PAD + jnp.arange(N_META)
    d_band = t_q[:, :, None] - t_band[:, None, :]
    d_meta = t_q[:, :, None] - t_meta[None, None, :]
    ok_band = (d_band >= 0) & (d_band < WINDOW) & (t_band[:, None, :] >= BLOCK)
    ok_meta = d_meta >= 0
    m = slopes.reshape(N_KV_HEADS, Q_GROUP, 1, 1, 1)
    l_band = jnp.where(ok_band, s_band - m * d_band.astype(jnp.float32), NEG_INF)
    l_meta = jnp.where(ok_meta, s_meta - m * d_meta.astype(jnp.float32), NEG_INF)
    l_sink = jnp.broadcast_to(sinks.astype(jnp.float32).reshape(1, N_KV_HEADS, Q_GROUP, 1, 1, 1),
                              l_band.shape[:-1] + (1,))
    probs = jax.nn.softmax(jnp.concatenate([l_band, l_meta, l_sink], axis=-1), axis=-1)
    p_band = probs[..., :2 * BLOCK].astype(v.dtype)
    p_meta = probs[..., 2 * BLOCK:2 * BLOCK + N_META].astype(v.dtype)
    out = (jnp.einsum('bkgnqs,bnskd->bnqkgd', p_band, vb)
           + jnp.einsum('bkgnqm,bmkd->bnqkgd', p_meta, vm))
    return out.reshape(B, P, ATTN_WIDTH)[:, META_PAD:]


def causal_mean_minus_self(u, w):
    L = u.shape[1]
    cs = jnp.cumsum(u, axis=1)
    lagged = jnp.pad(cs[:, :L - w], ((0, 0), (w, 0), (0, 0)))
    count = jnp.minimum(jnp.arange(1, L + 1), w).astype(jnp.float32)[None, :, None]
    return (cs - lagged) / count - u


def multiscale_pool(u):
    uf = u.astype(jnp.float32)
    groups = [causal_mean_minus_self(uf[..., g * POOL_GROUP_DIM:(g + 1) * POOL_GROUP_DIM], POOL_WINDOWS[g])
              for g in range(N_POOL_GROUPS)]
    return jnp.concatenate(groups, axis=-1).astype(u.dtype)


def hybrid_mixer(h, gain, w_in, sinks, w_attn_br, w_pool_grp, pool_scale, w_out, slopes):
    B, L = h.shape[0], h.shape[1]
    xn = rms_norm(h, gain)
    proj = xn @ w_in
    i0 = ATTN_WIDTH
    i1 = i0 + KV_WIDTH
    i2 = i1 + KV_WIDTH
    i3 = i2 + POOL_WIDTH
    q = proj[..., :i0].reshape(B, L, N_HEADS, HEAD_DIM)
    k = proj[..., i0:i1].reshape(B, L, N_KV_HEADS, HEAD_DIM)
    v = proj[..., i1:i2].reshape(B, L, N_KV_HEADS, HEAD_DIM)
    u = proj[..., i2:i3]
    gates = jax.nn.sigmoid(proj[..., i3:].astype(jnp.float32)).astype(h.dtype).reshape(B, L, N_BRANCHES, D_MODEL)
    a_branch = swa_sink_attention(q, k, v, sinks, slopes) @ w_attn_br
    pooled = multiscale_pool(u).reshape(B, L, N_POOL_GROUPS, POOL_GROUP_DIM)
    p_branch = jnp.einsum('blgc,gcd->blgd', pooled, w_pool_grp).reshape(B, L, D_MODEL) * pool_scale
    merged = gates[..., 0, :] * a_branch + gates[..., 1, :] * p_branch
    return merged @ w_out


def swiglu(x, wg, wu, wd):
    return (jax.nn.silu(x @ wg) * (x @ wu)) @ wd


def moe_swiglu(xn, router, wg, wu, wd):
    logits = (xn @ router).astype(jnp.float32)
    top_vals, top_idx = lax.top_k(logits, TOP_K)
    weights = jax.nn.softmax(top_vals, axis=-1)
    combine = jnp.sum(jax.nn.one_hot(top_idx, N_EXPERTS, dtype=jnp.float32) * weights[..., None], axis=-2)
    combine = combine.astype(xn.dtype)
    y = jnp.zeros_like(xn)
    for e in range(N_EXPERTS):
        y = y + combine[..., e:e + 1] * swiglu(xn, wg[e], wu[e], wd[e])
    return y


def setup_inputs(seed: int = 0) -> dict:
    key = jax.random.key(seed)
    ks = jax.random.split(key, 18)
    f32 = jnp.float32

    def nrm(k, shape, fan_in):
        return jax.random.normal(k, shape, f32) * (fan_in ** -0.5)

    return {
        "x": jax.random.normal(ks[0], (BATCH, SEQ, D_MODEL), f32),
        "meta_tokens": jax.random.normal(ks[1], (N_META, D_MODEL), f32),
        "norm_mix": 1.0 + 0.02 * jax.random.normal(ks[2], (DEPTH, D_MODEL), f32),
        "w_in": nrm(ks[3], (DEPTH, D_MODEL, IN_WIDTH), D_MODEL),
        "attn_sinks": 0.5 * jax.random.normal(ks[4], (DEPTH, N_HEADS), f32),
        "w_attn_br": nrm(ks[5], (DEPTH, ATTN_WIDTH, D_MODEL), ATTN_WIDTH),
        "w_pool_grp": nrm(ks[6], (DEPTH, N_POOL_GROUPS, POOL_GROUP_DIM, POOL_GROUP_OUT), POOL_GROUP_DIM),
        "pool_scale": 1.0 + 0.1 * jax.random.normal(ks[7], (DEPTH, D_MODEL), f32),
        "w_out": nrm(ks[8], (DEPTH, D_MODEL, D_MODEL), D_MODEL),
        "norm_ffn": 1.0 + 0.02 * jax.random.normal(ks[9], (DEPTH, D_MODEL), f32),
        "dense_w_gate": nrm(ks[10], (N_DENSE, D_MODEL, D_FF), D_MODEL),
        "dense_w_up": nrm(ks[11], (N_DENSE, D_MODEL, D_FF), D_MODEL),
        "dense_w_down": nrm(ks[12], (N_DENSE, D_FF, D_MODEL), D_FF),
        "moe_router": nrm(ks[13], (N_MOE, D_MODEL, N_EXPERTS), D_MODEL),
        "moe_w_gate": nrm(ks[14], (N_MOE, N_EXPERTS, D_MODEL, D_FF_EXPERT), D_MODEL),
        "moe_w_up": nrm(ks[15], (N_MOE, N_EXPERTS, D_MODEL, D_FF_EXPERT), D_MODEL),
        "moe_w_down": nrm(ks[16], (N_MOE, N_EXPERTS, D_FF_EXPERT, D_MODEL), D_FF_EXPERT),
        "norm_final": 1.0 + 0.02 * jax.random.normal(ks[17], (D_MODEL,), f32),
    }


def reference(x, meta_tokens, norm_mix, w_in, attn_sinks, w_attn_br, w_pool_grp, pool_scale, w_out,
              norm_ffn, dense_w_gate, dense_w_up, dense_w_down, moe_router, moe_w_gate, moe_w_up,
              moe_w_down, norm_final):
    B = x.shape[0]
    slopes = alibi_slopes()
    meta = jnp.broadcast_to(meta_tokens.astype(x.dtype)[None], (B, N_META, D_MODEL))
    h = jnp.concatenate([meta, x], axis=1)
    for layer in range(DEPTH):
        h = h + hybrid_mixer(h, norm_mix[layer], w_in[layer], attn_sinks[layer], w_attn_br[layer],
                             w_pool_grp[layer], pool_scale[layer], w_out[layer], slopes)
        hn = rms_norm(h, norm_ffn[layer])
        if layer % 2 == 0:
            j = layer // 2
            h = h + swiglu(hn, dense_w_gate[j], dense_w_up[j], dense_w_down[j])
        else:
            j = layer // 2
            h = h + moe_swiglu(hn, moe_router[j], moe_w_gate[j], moe_w_up[j], moe_w_down[j])
    return rms_norm(h, norm_final)[:, N_META:]
```

```python
import functools
import math

import numpy as np
import jax
import jax.numpy as jnp
from jax import lax
from jax.experimental import pallas as pl
from jax.experimental.pallas import tpu as pltpu

F32 = jnp.float32
BF16 = jnp.bfloat16

N_HEADS = 16
HEAD_DIM = 64
N_KV_HEADS = 2
KV_WIDTH = N_KV_HEADS * HEAD_DIM
WINDOW = 128
N_META = 16
POOL_WINDOWS = (2, 4, 8, 16)
N_POOL_GROUPS = 4
POOL_GROUP_DIM = 128
POOL_WIDTH = N_POOL_GROUPS * POOL_GROUP_DIM
N_EXPERTS = 8
RMS_EPS = 1e-5
NEG_BIAS = -1e30
LOG2E = 1.4426950408889634

LANES = 128
META_ROWS = 128
VMEM_LIMIT = 56 * 1024 * 1024

QB = 64
KB = WINDOW + QB
KCOLS = 256
SINK_COL = KB + N_META
POOL_SUB = 128


def _cparams(sem, vmem=VMEM_LIMIT):
    return pltpu.CompilerParams(dimension_semantics=sem, vmem_limit_bytes=vmem)


def _rms(x, gain):
    ms = jnp.mean(x * x, axis=-1, keepdims=True)
    return x * lax.rsqrt(ms + RMS_EPS) * gain


def _inproj_kernel(h_ref, gain_ref, w_ref, q_ref, kk_ref, vv_ref, u_ref, g_ref, *, d_model):
    xn = _rms(h_ref[...], gain_ref[...]).astype(BF16)
    aw = N_HEADS * HEAD_DIM
    qscale = (HEAD_DIM ** -0.5) * LOG2E
    half = aw // 2
    for c in range(2):
        q = jnp.dot(xn, w_ref[:, c * half:(c + 1) * half], preferred_element_type=F32)
        q_ref[:, c * half:(c + 1) * half] = (q * qscale).astype(BF16)
    kv = jnp.dot(xn, w_ref[:, aw:aw + 2 * KV_WIDTH], preferred_element_type=F32)
    k = kv[:, :KV_WIDTH]
    v = kv[:, KV_WIDTH:]
    kk_ref[:, :KV_WIDTH] = k.astype(BF16)
    kk_ref[:, KV_WIDTH:] = pltpu.roll(k, HEAD_DIM, 1).astype(BF16)
    vv_ref[:, :KV_WIDTH] = v.astype(BF16)
    vv_ref[:, KV_WIDTH:] = pltpu.roll(v, HEAD_DIM, 1).astype(BF16)
    o = aw + 2 * KV_WIDTH
    u_ref[...] = jnp.dot(xn, w_ref[:, o:o + POOL_WIDTH], preferred_element_type=F32).astype(BF16)
    o += POOL_WIDTH
    gw = 2 * d_model
    gc = 512
    for c in range(gw // gc):
        g = jnp.dot(xn, w_ref[:, o + c * gc:o + (c + 1) * gc], preferred_element_type=F32)
        g_ref[:, c * gc:(c + 1) * gc] = jax.nn.sigmoid(g).astype(BF16)


def _inproj(h, gain, w_bf, tm):
    t, d = h.shape
    n = w_bf.shape[1]
    aw = N_HEADS * HEAD_DIM
    row = lambda w: pl.BlockSpec((tm, w), lambda i: (i, 0))
    return pl.pallas_call(
        functools.partial(_inproj_kernel, d_model=d),
        grid=(t // tm,),
        in_specs=[row(d), pl.BlockSpec((1, d), lambda i: (0, 0)), pl.BlockSpec((d, n), lambda i: (0, 0))],
        out_specs=[row(aw), row(2 * KV_WIDTH), row(2 * KV_WIDTH), row(POOL_WIDTH), row(2 * d)],
        out_shape=[
            jax.ShapeDtypeStruct((t, aw), BF16),
            jax.ShapeDtypeStruct((t, 2 * KV_WIDTH), BF16),
            jax.ShapeDtypeStruct((t, 2 * KV_WIDTH), BF16),
            jax.ShapeDtypeStruct((t, POOL_WIDTH), BF16),
            jax.ShapeDtypeStruct((t, 2 * d), BF16),
        ],
        compiler_params=_cparams(("parallel",)),
        name="inproj",
    )(h, gain, w_bf)


def _attn_bias_tables():
    slopes = np.array([2.0 ** (-8.0 * (h + 1) / N_HEADS) for h in range(N_HEADS)], dtype=np.float64)
    i = np.arange(QB)[:, None]
    c = np.arange(KCOLS)[None, :]
    d_band = WINDOW + i - c
    ok_band = (c < KB) & (d_band >= 0) & (d_band < WINDOW)
    m = c - KB
    is_meta = (c >= KB) & (c < KB + N_META)
    d_meta = N_META + i - m
    tbl = np.full((4, N_HEADS, QB, KCOLS), NEG_BIAS, dtype=np.float64)
    for var in range(4):
        if var == 0:
            okb = ok_band
        elif var == 1:
            okb = ok_band & (c >= WINDOW)
        elif var == 2:
            okb = ok_band & (c >= WINDOW - QB)
        else:
            okb = np.zeros_like(ok_band)
        okm = np.broadcast_to(is_meta, (QB, KCOLS)) & ((d_meta - N_META >= 0) if var == 3 else True)
        okb = np.broadcast_to(okb, (QB, KCOLS))
        for h in range(N_HEADS):
            t = tbl[var, h]
            t[okb] = (-slopes[h] * LOG2E * d_band)[okb]
            t[okm] = (-slopes[h] * LOG2E * d_meta)[okm]
            t[:, SINK_COL] = 0.0
    tbl = tbl.reshape(4, N_HEADS // 2, 2, QB, KCOLS).transpose(0, 1, 3, 2, 4)
    tbl = tbl.reshape(4, N_HEADS // 2, QB, 2 * KCOLS)
    mrow = np.zeros((N_HEADS, KCOLS), dtype=np.float64)
    mrow[:, KB:KB + N_META] = (-slopes * LOG2E)[:, None]
    mrow = mrow.reshape(N_HEADS // 2, 1, 2 * KCOLS)
    sink_pos = np.zeros((N_HEADS // 2, 1, 2 * KCOLS), dtype=np.float32)
    sink_pos[:, 0, SINK_COL] = 1.0
    sink_pos[:, 0, KCOLS + SINK_COL] = 1.0
    return tbl.astype(np.float32), mrow.astype(np.float32), sink_pos


def _attn_kernel(q_ref, kk_ref, vv_ref, kkh_ref, vvh_ref, kkm_ref, vvm_ref, bias_ref, mrow_ref,
                 o_ref, kcat, vcat, bmat, vmat, *, tq, meta_mode):
    j = pl.program_id(1)
    hd = HEAD_DIM

    kcat[0:WINDOW, :] = kkh_ref[...]
    kcat[WINDOW:, :] = kk_ref[...]
    vcat[0:WINDOW, :] = vvh_ref[...]
    vcat[WINDOW:, :] = vv_ref[...]

    def place(dst, src, rows, kvh, row0, nrows):
        lo_col = 0 if kvh == 0 else 2 * hd
        hi_col = 3 * hd if kvh == 0 else hd
        dst[kvh, row0:row0 + nrows, 0:hd] = src[rows, lo_col:lo_col + hd]
        dst[kvh, KCOLS + row0:KCOLS + row0 + nrows, hd:2 * hd] = src[rows, hi_col:hi_col + hd]

    bmat[...] = jnp.zeros_like(bmat)
    for kvh in range(N_KV_HEADS):
        vmat[kvh, :, 0:LANES] = jnp.zeros((2 * KCOLS, LANES), BF16)
        vmat[kvh, 0:KCOLS, LANES:LANES + hd] = jnp.ones((KCOLS, hd), BF16)
        vmat[kvh, 0:KCOLS, LANES + hd:2 * LANES] = jnp.zeros((KCOLS, hd), BF16)
        vmat[kvh, KCOLS:2 * KCOLS, LANES:LANES + hd] = jnp.zeros((KCOLS, hd), BF16)
        vmat[kvh, KCOLS:2 * KCOLS, LANES + hd:2 * LANES] = jnp.ones((KCOLS, hd), BF16)
        place(bmat, kkm_ref, slice(None), kvh, KB, N_META)
        place(vmat, vvm_ref, slice(None), kvh, KB, N_META)

    def half_block(s, carry):
        o = pl.multiple_of(s * QB, QB)
        if meta_mode:
            var = 3
            pos0 = jnp.asarray(s * QB - N_META, dtype=F32)
        else:
            first = jnp.logical_and(j == 0, s < 2)
            var = jnp.where(first, s + 1, 0)
            pos0 = (j * tq + s * QB).astype(F32)
        for kvh in range(N_KV_HEADS):
            place(bmat, kcat, pl.ds(o, KB), kvh, 0, KB)
            place(vmat, vcat, pl.ds(o, KB), kvh, 0, KB)
        for p in range(N_HEADS // 2):
            kvh = p // (N_HEADS // 2 // N_KV_HEADS)
            qp = q_ref[pl.ds(o, QB), p * LANES:(p + 1) * LANES]
            sc = lax.dot_general(qp, bmat[kvh], (((1,), (1,)), ((), ())), preferred_element_type=F32)
            sc = sc + (bias_ref[var, p] + mrow_ref[p] * pos0)
            m0 = jnp.max(sc[:, :KCOLS], axis=1, keepdims=True)
            m1 = jnp.max(sc[:, KCOLS:], axis=1, keepdims=True)
            pr = jnp.concatenate([jnp.exp2(sc[:, :KCOLS] - m0), jnp.exp2(sc[:, KCOLS:] - m1)], axis=1)
            ov = jnp.dot(pr.astype(BF16), vmat[kvh], preferred_element_type=F32)
            o_ref[pl.ds(o, QB), p * LANES:(p + 1) * LANES] = (ov[:, :LANES] / ov[:, LANES:]).astype(BF16)
        return carry

    lax.fori_loop(0, tq // QB, half_block, 0)


def _attention(q, kk, vv, kkm, vvm, bias, mrow, *, batch, seq, tq, meta_mode):
    aw = N_HEADS * HEAD_DIM
    nj = seq // tq
    hb = tq // WINDOW if not meta_mode else 1

    def row_map(b, j):
        return (b * nj + j, 0)

    def halo_map(b, j):
        if meta_mode:
            return (0, 0)
        return (jnp.maximum(b * nj * hb + j * hb - 1, b * nj * hb), 0)

    const2 = lambda b, j: (0, 0)
    return pl.pallas_call(
        functools.partial(_attn_kernel, tq=tq, meta_mode=meta_mode),
        grid=(batch, nj),
        in_specs=[
            pl.BlockSpec((tq, aw), row_map),
            pl.BlockSpec((tq, 2 * KV_WIDTH), row_map),
            pl.BlockSpec((tq, 2 * KV_WIDTH), row_map),
            pl.BlockSpec((WINDOW, 2 * KV_WIDTH), halo_map),
            pl.BlockSpec((WINDOW, 2 * KV_WIDTH), halo_map),
            pl.BlockSpec((N_META, 2 * KV_WIDTH), const2),
            pl.BlockSpec((N_META, 2 * KV_WIDTH), const2),
            pl.BlockSpec(bias.shape, lambda b, j: (0, 0, 0, 0)),
            pl.BlockSpec(mrow.shape, lambda b, j: (0, 0, 0)),
        ],
        out_specs=pl.BlockSpec((tq, aw), row_map),
        out_shape=jax.ShapeDtypeStruct((batch * seq, aw), BF16),
        scratch_shapes=[
            pltpu.VMEM((WINDOW + tq, 2 * KV_WIDTH), BF16),
            pltpu.VMEM((WINDOW + tq, 2 * KV_WIDTH), BF16),
            pltpu.VMEM((N_KV_HEADS, 2 * KCOLS, LANES), BF16),
            pltpu.VMEM((N_KV_HEADS, 2 * KCOLS, 2 * LANES), BF16),
        ],
        compiler_params=_cparams(("parallel", "parallel")),
        name="attn_meta" if meta_mode else "attn",
    )(q, kk, vv, kk, vv, kkm, vvm, bias, mrow)


def _pool_matrices(meta):
    a = np.zeros((N_POOL_GROUPS, POOL_SUB, 2 * POOL_SUB), dtype=np.float32)
    for g, w in enumerate(POOL_WINDOWS):
        for t in range(POOL_SUB):
            lo = t - w + 1
            if meta:
                lo = max(lo, 0)
            cnt = t - lo + 1
            a[g, t, POOL_SUB + lo:POOL_SUB + t + 1] = 1.0 / cnt
            a[g, t, POOL_SUB + t] -= 1.0
    return a


def _merge_kernel(*refs, tm, with_router):
    (attn_ref, u_ref, uh_ref, um_ref, g_ref, h_ref, wa_ref, wp_ref, ps_ref, wo_ref, pa_ref,
     gain_ref) = refs[:12]
    refs = refs[12:]
    if with_router:
        rt_ref = refs[0]
        refs = refs[1:]
    hout_ref, hn_ref = refs[:2]
    refs = refs[2:]
    if with_router:
        ri_ref, rw_ref = refs[:2]
        refs = refs[2:]
    ucat, = refs

    j = pl.program_id(1)
    d = h_ref.shape[1]
    ucat[0:POOL_SUB, :] = jnp.where(j == 0, um_ref[...], uh_ref[...])
    ucat[POOL_SUB:, :] = u_ref[...]

    a = jnp.dot(attn_ref[...], wa_ref[...], preferred_element_type=F32)
    for r in range(tm // POOL_SUB):
        rows = slice(r * POOL_SUB, (r + 1) * POOL_SUB)
        parts = []
        for g in range(N_POOL_GROUPS):
            uw = ucat[r * POOL_SUB:(r + 2) * POOL_SUB, g * POOL_GROUP_DIM:(g + 1) * POOL_GROUP_DIM]
            pooled = jnp.dot(pa_ref[g], uw, preferred_element_type=F32).astype(BF16)
            parts.append(jnp.dot(pooled, wp_ref[g], preferred_element_type=F32))
        pb = jnp.concatenate(parts, axis=1) * ps_ref[...]
        gates = g_ref[rows, :]
        merged = gates[:, :d].astype(F32) * a[rows, :] + gates[:, d:].astype(F32) * pb
        out = jnp.dot(merged.astype(BF16), wo_ref[...], preferred_element_type=F32)
        hnew = h_ref[rows, :] + out
        hout_ref[rows, :] = hnew
        xn = _rms(hnew, gain_ref[...])
        xh = xn.astype(BF16)
        hn_ref[rows, :] = xn.astype(hn_ref.dtype)
        if with_router:
            xl = (xn - xh.astype(F32)).astype(BF16)
            nt = (((1,), (1,)), ((), ()))
            t_hi = lax.dot_general(rt_ref[...], xh, nt, preferred_element_type=F32)
            t_lo = lax.dot_general(rt_ref[...], xl, nt, preferred_element_type=F32)
            lg = t_hi[:N_EXPERTS] + t_hi[N_EXPERTS:] + t_lo[:N_EXPERTS]
            eid = lax.broadcasted_iota(jnp.int32, lg.shape, 0).astype(F32)
            m1 = jnp.max(lg, axis=0, keepdims=True)
            i1 = jnp.min(jnp.where(lg == m1, eid, float(N_EXPERTS)), axis=0, keepdims=True)
            lg2 = jnp.where(eid == i1, -jnp.inf, lg)
            m2 = jnp.max(lg2, axis=0, keepdims=True)
            i2 = jnp.min(jnp.where(lg2 == m2, eid, float(N_EXPERTS)), axis=0, keepdims=True)
            e2 = jnp.exp(m2 - m1)
            w1 = 1.0 / (1.0 + e2)
            w2 = e2 / (1.0 + e2)
            row = lax.broadcasted_iota(jnp.int32, lg.shape, 0)
            ri_ref[:, rows] = jnp.where(row == 0, i1, jnp.where(row == 1, i2, 0.0)).astype(jnp.int32)
            rw_ref[:, rows] = jnp.where(row == 0, w1, jnp.where(row == 1, w2, 0.0))


def _merge(attn, u, u_meta, gates, h, wa, wp, pscale, wo, pool_a, gain, rt, *, batch, seq, tm):
    t, d = h.shape
    nj = seq // tm
    hs = tm // POOL_SUB
    with_router = rt is not None

    def row_map(b, j):
        return (b * nj + j, 0)

    def halo_map(b, j):
        return (jnp.maximum((b * nj + j) * hs - 1, 0), 0)

    c2 = lambda b, j: (0, 0)
    c3 = lambda b, j: (0, 0, 0)
    in_specs = [
        pl.BlockSpec((tm, attn.shape[1]), row_map),
        pl.BlockSpec((tm, POOL_WIDTH), row_map),
        pl.BlockSpec((POOL_SUB, POOL_WIDTH), halo_map),
        pl.BlockSpec((POOL_SUB, POOL_WIDTH), c2),
        pl.BlockSpec((tm, 2 * d), row_map),
        pl.BlockSpec((tm, d), row_map),
        pl.BlockSpec(wa.shape, c2),
        pl.BlockSpec(wp.shape, c3),
        pl.BlockSpec((1, d), c2),
        pl.BlockSpec(wo.shape, c2),
        pl.BlockSpec(pool_a.shape, c3),
        pl.BlockSpec((1, d), c2),
    ]
    args = [attn, u, u, u_meta, gates, h, wa, wp, pscale, wo, pool_a, gain]
    out_specs = [pl.BlockSpec((tm, d), row_map), pl.BlockSpec((tm, d), row_map)]
    out_shape = [jax.ShapeDtypeStruct((t, d), F32), jax.ShapeDtypeStruct((t, d), F32 if with_router else BF16)]
    if with_router:
        in_specs.append(pl.BlockSpec(rt.shape, c2))
        args.append(rt)
        lane_map = lambda b, j: (0, b * nj + j)
        out_specs += [pl.BlockSpec((8, tm), lane_map), pl.BlockSpec((8, tm), lane_map)]
        out_shape += [jax.ShapeDtypeStruct((8, t), jnp.int32), jax.ShapeDtypeStruct((8, t), F32)]
    return pl.pallas_call(
        functools.partial(_merge_kernel, tm=tm, with_router=with_router),
        grid=(batch, nj),
        in_specs=in_specs,
        out_specs=out_specs,
        out_shape=out_shape,
        scratch_shapes=[pltpu.VMEM((POOL_SUB + tm, POOL_WIDTH), BF16)],
        compiler_params=_cparams(("parallel", "parallel")),
        name="merge_router" if with_router else "merge",
    )(*args)


def _swiglu_act(g, u):
    return (g * jax.nn.sigmoid(g)) * u


def _dense_ffn_kernel(hn_ref, h_ref, wg_ref, wu_ref, wd_ref, o_ref, *, fc):
    x = hn_ref[...]
    acc = h_ref[...]
    for c in range(wg_ref.shape[1] // fc):
        cols = slice(c * fc, (c + 1) * fc)
        g = jnp.dot(x, wg_ref[:, cols], preferred_element_type=F32)
        u = jnp.dot(x, wu_ref[:, cols], preferred_element_type=F32)
        acc = acc + jnp.dot(_swiglu_act(g, u).astype(BF16), wd_ref[cols, :], preferred_element_type=F32)
    o_ref[...] = acc


def _dense_ffn(hn, h, wg, wu, wd, tm, fc):
    t, d = h.shape
    row = pl.BlockSpec((tm, d), lambda i: (i, 0))
    c2 = lambda i: (0, 0)
    return pl.pallas_call(
        functools.partial(_dense_ffn_kernel, fc=fc),
        grid=(t // tm,),
        in_specs=[row, row, pl.BlockSpec(wg.shape, c2), pl.BlockSpec(wu.shape, c2), pl.BlockSpec(wd.shape, c2)],
        out_specs=row,
        out_shape=jax.ShapeDtypeStruct((t, d), F32),
        compiler_params=_cparams(("parallel",)),
        name="dense_ffn",
    )(hn, h, wg, wu, wd)


E_ROWS = 16


def _route_pos_kernel(ri_ref, tri_ref, pos_ref, te_ref, carry, starts, *, tn, tmg, nt_pad):
    ph = pl.program_id(0)
    i = pl.program_id(1)
    eid = lax.broadcasted_iota(jnp.int32, (E_ROWS, tn), 0)
    ri = ri_ref[...]
    oh0 = (eid == ri[0:1, :]).astype(F32)
    oh1 = (eid == ri[1:2, :]).astype(F32)
    oh = oh0 + oh1

    @pl.when(jnp.logical_and(ph == 0, i == 0))
    def _():
        carry[...] = jnp.zeros_like(carry)

    @pl.when(ph == 0)
    def _():
        carry[...] += jnp.sum(oh, axis=1, keepdims=True)
        pos_ref[...] = jnp.zeros_like(pos_ref)

    @pl.when(jnp.logical_and(ph == 1, i == 0))
    def _():
        cnt = carry[...]
        padded = jnp.ceil(cnt / tmg) * tmg
        r = lax.broadcasted_iota(jnp.int32, (E_ROWS, E_ROWS), 0)
        c = lax.broadcasted_iota(jnp.int32, (E_ROWS, E_ROWS), 1)
        pad_row = jnp.sum(jnp.where(r == c, padded, 0.0), axis=0, keepdims=True)
        st = jnp.sum(jnp.where(c < r, pad_row, 0.0), axis=1, keepdims=True)
        starts[...] = st
        ends = st + padded
        tile0 = (lax.broadcasted_iota(jnp.int32, (E_ROWS, nt_pad), 1) * tmg).astype(F32)
        texp = jnp.sum((tile0 >= ends).astype(F32), axis=0, keepdims=True)
        texp = jnp.minimum(texp, N_EXPERTS - 1.0)
        nused = jnp.max(ends, axis=0, keepdims=True) / tmg
        row = lax.broadcasted_iota(jnp.int32, (8, nt_pad), 0)
        te_ref[...] = jnp.where(row == 0, texp, jnp.where(row == 1, nused, 0.0)).astype(jnp.int32)
        carry[...] = jnp.zeros_like(carry)

    @pl.when(ph == 1)
    def _():
        cum = jnp.dot(oh.astype(BF16), tri_ref[...], preferred_element_type=F32)
        base = starts[...] + carry[...] + cum
        p0 = jnp.sum(oh0 * base, axis=0, keepdims=True)
        p1 = jnp.sum(oh1 * base, axis=0, keepdims=True)
        row = lax.broadcasted_iota(jnp.int32, (8, tn), 0)
        pos_ref[...] = jnp.where(row == 0, p0, jnp.where(row == 1, p1, 0.0)).astype(jnp.int32)
        carry[...] += jnp.sum(oh, axis=1, keepdims=True)


def _route_positions(ri, tn, tmg, nt_pad):
    t = ri.shape[1]
    nb = t // tn
    tri = jnp.asarray(np.triu(np.ones((tn, tn), dtype=np.float32), k=1), dtype=BF16)
    pos_map = lambda p, i: (0, jnp.where(p == 0, nb, i))
    return pl.pallas_call(
        functools.partial(_route_pos_kernel, tn=tn, tmg=tmg, nt_pad=nt_pad),
        grid=(2, nb),
        in_specs=[pl.BlockSpec((8, tn), lambda p, i: (0, i)), pl.BlockSpec((tn, tn), lambda p, i: (0, 0))],
        out_specs=[pl.BlockSpec((8, tn), pos_map), pl.BlockSpec((8, nt_pad), lambda p, i: (0, 0))],
        out_shape=[jax.ShapeDtypeStruct((8, t + tn), jnp.int32), jax.ShapeDtypeStruct((8, nt_pad), jnp.int32)],
        scratch_shapes=[pltpu.VMEM((E_ROWS, 1), F32), pltpu.VMEM((E_ROWS, 1), F32)],
        compiler_params=_cparams(("arbitrary", "arbitrary")),
        name="route_pos",
    )(ri, tri)


def _row_copy(src, dst, sem):
    return pltpu.make_async_copy(src, dst, sem)


def _scatter_kernel(pos_hbm, x_ref, zero_hbm, xs_hbm, idx, sem_i, sem, *, tm):
    del zero_hbm
    i = pl.program_id(0)
    cp = pltpu.make_async_copy(pos_hbm.at[i], idx, sem_i)
    cp.start()
    cp.wait()

    def issue(r, c):
        for k in range(2):
            _row_copy(x_ref.at[pl.ds(r, 1)], xs_hbm.at[pl.ds(idx[k * tm + r], 1)], sem).start()
        return c

    lax.fori_loop(0, tm, issue, 0)

    def drain(r, c):
        for k in range(2):
            _row_copy(x_ref.at[pl.ds(0, 1)], xs_hbm.at[pl.ds(0, 1)], sem).wait()
        return c

    lax.fori_loop(0, tm, drain, 0)


def _scatter_rows(pos_tiles, x, n_sorted, tm):
    t, d = x.shape
    zeros = jnp.zeros((n_sorted, d), x.dtype)
    return pl.pallas_call(
        functools.partial(_scatter_kernel, tm=tm),
        grid=(t // tm,),
        in_specs=[pl.BlockSpec(memory_space=pl.ANY), pl.BlockSpec((tm, d), lambda i: (i, 0)),
                  pl.BlockSpec(memory_space=pl.ANY)],
        out_specs=pl.BlockSpec(memory_space=pl.ANY),
        out_shape=jax.ShapeDtypeStruct((n_sorted, d), x.dtype),
        scratch_shapes=[pltpu.SMEM((2 * tm,), jnp.int32), pltpu.SemaphoreType.DMA, pltpu.SemaphoreType.DMA],
        input_output_aliases={2: 0},
        compiler_params=_cparams(("arbitrary",)),
        name="moe_scatter",
    )(pos_tiles, x, zeros)


def _grouped_ffn_kernel(te_ref, nu_ref, x_ref, wg_ref, wu_ref, wd_ref, y_ref, xb, acc):
    i = pl.program_id(0)
    j = pl.program_id(1)
    nj = pl.num_programs(1)

    @pl.when(i < nu_ref[0])
    def _():
        @pl.when(j == 0)
        def _():
            xb[...] = x_ref[...].astype(BF16)
            acc[...] = jnp.zeros_like(acc)

        x = xb[...]
        g = jnp.dot(x, wg_ref[...], preferred_element_type=F32)
        u = jnp.dot(x, wu_ref[...], preferred_element_type=F32)
        acc[...] += jnp.dot(_swiglu_act(g, u).astype(BF16), wd_ref[...], preferred_element_type=F32)

        @pl.when(j == nj - 1)
        def _():
            y_ref[...] = acc[...]

    @pl.when(jnp.logical_and(i >= nu_ref[0], j == nj - 1))
    def _():
        y_ref[...] = jnp.zeros_like(y_ref)


def _grouped_ffn(te, nused, xs, wg, wu, wd, tmg, fc, nt):
    n, d = xs.shape
    f = wg.shape[2]
    nj = f // fc

    def tile(i, nu):
        return jnp.minimum(i, nu[0] - 1)

    def chunk(i, j, nu):
        return jnp.where(i < nu[0], j, nj - 1)

    grid_spec = pltpu.PrefetchScalarGridSpec(
        num_scalar_prefetch=2,
        grid=(nt, nj),
        in_specs=[
            pl.BlockSpec((tmg, d), lambda i, j, te, nu: (tile(i, nu), 0)),
            pl.BlockSpec((None, d, fc), lambda i, j, te, nu: (te[tile(i, nu)], 0, chunk(i, j, nu))),
            pl.BlockSpec((None, d, fc), lambda i, j, te, nu: (te[tile(i, nu)], 0, chunk(i, j, nu))),
            pl.BlockSpec((None, fc, d), lambda i, j, te, nu: (te[tile(i, nu)], chunk(i, j, nu), 0)),
        ],
        out_specs=pl.BlockSpec((tmg, d), lambda i, j, te, nu: (i, 0)),
        scratch_shapes=[pltpu.VMEM((tmg, d), BF16), pltpu.VMEM((tmg, d), F32)],
    )
    return pl.pallas_call(
        _grouped_ffn_kernel,
        grid_spec=grid_spec,
        out_shape=jax.ShapeDtypeStruct((n, d), F32),
        compiler_params=_cparams(("arbitrary", "arbitrary")),
        name="moe_ffn",
    )(te, nused, xs, wg, wu, wd)


def _combine_kernel(pos_hbm, ys_hbm, h_ref, rw_ref, gain_ref, o_ref, idx, ybuf, sem_i, sem, *, tm):
    i = pl.program_id(0)
    cp = pltpu.make_async_copy(pos_hbm.at[i], idx, sem_i)
    cp.start()
    cp.wait()

    def issue(r, c):
        for k in range(2):
            _row_copy(ys_hbm.at[pl.ds(idx[k * tm + r], 1)], ybuf.at[k, pl.ds(r, 1)], sem).start()
        return c

    lax.fori_loop(0, tm, issue, 0)

    def drain(r, c):
        for k in range(2):
            _row_copy(ys_hbm.at[pl.ds(0, 1)], ybuf.at[0, pl.ds(0, 1)], sem).wait()
        return c

    lax.fori_loop(0, tm, drain, 0)
    w = rw_ref[...]
    y = h_ref[...] + w[:, 0:1] * ybuf[0] + w[:, 1:2] * ybuf[1]
    o_ref[...] = _rms(y, gain_ref[...])


def _combine(pos_tiles, ys, h, rw_tok, gain, tm):
    t, d = h.shape
    return pl.pallas_call(
        functools.partial(_combine_kernel, tm=tm),
        grid=(t // tm,),
        in_specs=[pl.BlockSpec(memory_space=pl.ANY), pl.BlockSpec(memory_space=pl.ANY),
                  pl.BlockSpec((tm, d), lambda i: (i, 0)),
                  pl.BlockSpec((tm, LANES), lambda i: (i, 0)),
                  pl.BlockSpec((1, d), lambda i: (0, 0))],
        out_specs=pl.BlockSpec((tm, d), lambda i: (i, 0)),
        out_shape=jax.ShapeDtypeStruct((t, d), F32),
        scratch_shapes=[pltpu.SMEM((2 * tm,), jnp.int32), pltpu.VMEM((2, tm, d), F32),
                        pltpu.SemaphoreType.DMA, pltpu.SemaphoreType.DMA],
        compiler_params=_cparams(("arbitrary",)),
        name="moe_combine",
    )(pos_tiles, ys, h, rw_tok, gain)


def _half_if_aligned(n):
    return n // 2 if n % (2 * LANES) == 0 else n


def _tiles(seq, d_ff, d_ff_expert):
    return dict(
        tm_in=512,
        tq=min(1024, seq),
        tm_mg=256,
        tm_ffn=512, fc_ffn=_half_if_aligned(d_ff),
        tn_route=1024,
        tm_sc=256,
        tmg=512, fc_moe=_half_if_aligned(d_ff_expert),
    )


def kernel(x, meta_tokens, norm_mix, w_in, attn_sinks, w_attn_br, w_pool_grp, pool_scale, w_out, norm_ffn,
           dense_w_gate, dense_w_up, dense_w_down, moe_router, moe_w_gate, moe_w_up, moe_w_down, norm_final):
    batch, seq, d = x.shape
    depth = w_in.shape[0]
    assert depth == 2 and dense_w_gate.shape[0] == 1 and moe_router.shape[0] == 1, "dense layer then expert layer"
    assert moe_router.shape[2] == N_EXPERTS and w_in.shape[2] == N_HEADS * HEAD_DIM + 2 * KV_WIDTH + POOL_WIDTH + 2 * d
    t = batch * seq
    cfg = _tiles(seq, dense_w_gate.shape[2], moe_w_gate.shape[3])
    bf = lambda a: a.astype(BF16)

    bias_np, mrow_np, sinkpos_np = _attn_bias_tables()
    mrow = jnp.asarray(mrow_np)
    pool_a_main = jnp.asarray(_pool_matrices(False), dtype=BF16)
    pool_a_meta = jnp.asarray(_pool_matrices(True), dtype=BF16)

    h = x.reshape(t, d)
    hm = jnp.concatenate([meta_tokens.astype(F32), jnp.zeros((META_ROWS - N_META, d), F32)], axis=0)

    out = None
    for layer in range(depth):
        gain = norm_mix[layer].reshape(1, d)
        w_in_bf = bf(w_in[layer])
        sink_row = jnp.repeat(attn_sinks[layer].astype(F32) * LOG2E, KCOLS).reshape(N_HEADS // 2, 1, 2 * KCOLS)
        bias = jnp.asarray(bias_np) + (jnp.asarray(sinkpos_np) * sink_row)[None]
        wa, wo = bf(w_attn_br[layer]), bf(w_out[layer])
        wp = bf(w_pool_grp[layer])
        ps = pool_scale[layer].reshape(1, d)
        gain_ffn = norm_ffn[layer].reshape(1, d)

        qm, kkm_all, vvm_all, um, gm = _inproj(hm, gain, w_in_bf, META_ROWS)
        kkm, vvm = kkm_all[:N_META], vvm_all[:N_META]
        um16 = um[:N_META]
        q, kk, vv, u, g = _inproj(h, gain, w_in_bf, cfg["tm_in"])
        attn = _attention(q, kk, vv, kkm, vvm, bias, mrow, batch=batch, seq=seq, tq=cfg["tq"], meta_mode=False)

        u_before = jnp.concatenate([jnp.zeros((POOL_SUB - N_META, POOL_WIDTH), BF16), um16], axis=0)
        if layer == 0:
            attn_m = _attention(qm, kkm_all, vvm_all, kkm, vvm, bias, mrow,
                                batch=1, seq=META_ROWS, tq=META_ROWS, meta_mode=True)
            hm, hnm = _merge(attn_m, um, jnp.zeros_like(u_before), gm, hm, wa, wp, ps, wo, pool_a_meta, gain_ffn,
                             None, batch=1, seq=META_ROWS, tm=META_ROWS)
            h, hn = _merge(attn, u, u_before, g, h, wa, wp, ps, wo, pool_a_main, gain_ffn, None,
                           batch=batch, seq=seq, tm=cfg["tm_mg"])
            wg, wu, wd = bf(dense_w_gate[0]), bf(dense_w_up[0]), bf(dense_w_down[0])
            hm = _dense_ffn(hnm, hm, wg, wu, wd, META_ROWS, cfg["fc_ffn"])
            h = _dense_ffn(hn, h, wg, wu, wd, cfg["tm_ffn"], cfg["fc_ffn"])
        else:
            r = moe_router[0].astype(F32)
            r_hi = r.astype(BF16)
            r_lo = (r - r_hi.astype(F32)).astype(BF16)
            rt = jnp.concatenate([r_hi.T, r_lo.T], axis=0)
            h, hn, ri, rw = _merge(attn, u, u_before, g, h, wa, wp, ps, wo, pool_a_main, gain_ffn, rt,
                                   batch=batch, seq=seq, tm=cfg["tm_mg"])
            wg, wu, wd = bf(moe_w_gate[0]), bf(moe_w_up[0]), bf(moe_w_down[0])
            out = _moe(h, hn, ri, rw, wg, wu, wd, norm_final.reshape(1, d), cfg)
    return out.reshape(batch, seq, d)


def _moe(h, hn, ri, rw, wg, wu, wd, gain_final, cfg):
    t, d = h.shape
    tmg, tm = cfg["tmg"], cfg["tm_sc"]
    nt = 2 * t // tmg + N_EXPERTS
    nt_pad = -(-nt // LANES) * LANES
    pos, te = _route_positions(ri, cfg["tn_route"], tmg, nt_pad)
    pos_tiles = pos[:2, :t].reshape(2, t // tm, tm).transpose(1, 0, 2).reshape(t // tm, 2 * tm)
    xs = _scatter_rows(pos_tiles, hn, nt * tmg, tm)
    ys = _grouped_ffn(te[0, :nt], te[1, :1], xs, wg, wu, wd, tmg, cfg["fc_moe"], nt)
    rw_tok = jnp.pad(rw[:2].T, ((0, 0), (0, LANES - 2)))
    return _combine(pos_tiles, ys, h, rw_tok, gain_final, tm)
```

```python
import functools
import math

import numpy as np
import jax
import jax.numpy as jnp
from jax import lax
from jax.experimental import pallas as pl
from jax.experimental.pallas import tpu as pltpu

F32 = jnp.float32
BF16 = jnp.bfloat16

N_HEADS = 16
HEAD_DIM = 64
N_KV_HEADS = 2
KV_WIDTH = N_KV_HEADS * HEAD_DIM
WINDOW = 128
N_META = 16
POOL_WINDOWS = (2, 4, 8, 16)
N_POOL_GROUPS = 4
POOL_GROUP_DIM = 128
POOL_WIDTH = N_POOL_GROUPS * POOL_GROUP_DIM
N_EXPERTS = 8
RMS_EPS = 1e-5
NEG_BIAS = -1e30
LOG2E = 1.4426950408889634

LANES = 128
META_ROWS = 128
VMEM_LIMIT = 56 * 1024 * 1024

QB = 64
KB = WINDOW + QB
KCOLS = 256
SINK_COL = KB + N_META
POOL_SUB = 128


def _cparams(sem, vmem=VMEM_LIMIT):
    return pltpu.CompilerParams(dimension_semantics=sem, vmem_limit_bytes=vmem)


def _rms(x, gain):
    ms = jnp.mean(x * x, axis=-1, keepdims=True)
    return x * lax.rsqrt(ms + RMS_EPS) * gain


def _inproj_kernel(h_ref, gain_ref, w_ref, q_ref, kk_ref, vv_ref, u_ref, g_ref, *, d_model):
    xn = _rms(h_ref[...], gain_ref[...]).astype(BF16)
    aw = N_HEADS * HEAD_DIM
    qscale = (HEAD_DIM ** -0.5) * LOG2E
    half = aw // 2
    for c in range(2):
        q = jnp.dot(xn, w_ref[:, c * half:(c + 1) * half], preferred_element_type=F32)
        q_ref[:, c * half:(c + 1) * half] = (q * qscale).astype(BF16)
    kv = jnp.dot(xn, w_ref[:, aw:aw + 2 * KV_WIDTH], preferred_element_type=F32)
    k = kv[:, :KV_WIDTH]
    v = kv[:, KV_WIDTH:]
    kk_ref[:, :KV_WIDTH] = k.astype(BF16)
    kk_ref[:, KV_WIDTH:] = pltpu.roll(k, HEAD_DIM, 1).astype(BF16)
    vv_ref[:, :KV_WIDTH] = v.astype(BF16)
    vv_ref[:, KV_WIDTH:] = pltpu.roll(v, HEAD_DIM, 1).astype(BF16)
    o = aw + 2 * KV_WIDTH
    u_ref[...] = jnp.dot(xn, w_ref[:, o:o + POOL_WIDTH], preferred_element_type=F32).astype(BF16)
    o += POOL_WIDTH
    gw = 2 * d_model
    gc = 512
    for c in range(gw // gc):
        g = jnp.dot(xn, w_ref[:, o + c * gc:o + (c + 1) * gc], preferred_element_type=F32)
        g_ref[:, c * gc:(c + 1) * gc] = jax.nn.sigmoid(g).astype(BF16)


def _inproj(h, gain, w_bf, tm):
    t, d = h.shape
    n = w_bf.shape[1]
    aw = N_HEADS * HEAD_DIM
    row = lambda w: pl.BlockSpec((tm, w), lambda i: (i, 0))
    return pl.pallas_call(
        functools.partial(_inproj_kernel, d_model=d),
        grid=(t // tm,),
        in_specs=[row(d), pl.BlockSpec((1, d), lambda i: (0, 0)), pl.BlockSpec((d, n), lambda i: (0, 0))],
        out_specs=[row(aw), row(2 * KV_WIDTH), row(2 * KV_WIDTH), row(POOL_WIDTH), row(2 * d)],
        out_shape=[
            jax.ShapeDtypeStruct((t, aw), BF16),
            jax.ShapeDtypeStruct((t, 2 * KV_WIDTH), BF16),
            jax.ShapeDtypeStruct((t, 2 * KV_WIDTH), BF16),
            jax.ShapeDtypeStruct((t, POOL_WIDTH), BF16),
            jax.ShapeDtypeStruct((t, 2 * d), BF16),
        ],
        compiler_params=_cparams(("parallel",)),
        name="inproj",
    )(h, gain, w_bf)


def _attn_bias_tables():
    slopes = np.array([2.0 ** (-8.0 * (h + 1) / N_HEADS) for h in range(N_HEADS)], dtype=np.float64)
    i = np.arange(QB)[:, None]
    c = np.arange(KCOLS)[None, :]
    d_band = WINDOW + i - c
    ok_band = (c < KB) & (d_band >= 0) & (d_band < WINDOW)
    m = c - KB
    is_meta = (c >= KB) & (c < KB + N_META)
    d_meta = N_META + i - m
    tbl = np.full((4, N_HEADS, QB, KCOLS), NEG_BIAS, dtype=np.float64)
    for var in range(4):
        if var == 0:
            okb = ok_band
        elif var == 1:
            okb = ok_band & (c >= WINDOW)
        elif var == 2:
            okb = ok_band & (c >= WINDOW - QB)
        else:
            okb = np.zeros_like(ok_band)
        okm = np.broadcast_to(is_meta, (QB, KCOLS)) & ((d_meta - N_META >= 0) if var == 3 else True)
        okb = np.broadcast_to(okb, (QB, KCOLS))
        for h in range(N_HEADS):
            t = tbl[var, h]
            t[okb] = (-slopes[h] * LOG2E * d_band)[okb]
            t[okm] = (-slopes[h] * LOG2E * d_meta)[okm]
            t[:, SINK_COL] = 0.0
    tbl = tbl.reshape(4, N_HEADS // 2, 2, QB, KCOLS).transpose(0, 1, 3, 2, 4)
    tbl = tbl.reshape(4, N_HEADS // 2, QB, 2 * KCOLS)
    mrow = np.zeros((N_HEADS, KCOLS), dtype=np.float64)
    mrow[:, KB:KB + N_META] = (-slopes * LOG2E)[:, None]
    mrow = mrow.reshape(N_HEADS // 2, 1, 2 * KCOLS)
    sink_pos = np.zeros((N_HEADS // 2, 1, 2 * KCOLS), dtype=np.float32)
    sink_pos[:, 0, SINK_COL] = 1.0
    sink_pos[:, 0, KCOLS + SINK_COL] = 1.0
    return tbl.astype(np.float32), mrow.astype(np.float32), sink_pos


def _attn_kernel(q_ref, kk_ref, vv_ref, kkh_ref, vvh_ref, kkm_ref, vvm_ref, bias_ref, mrow_ref,
                 o_ref, kcat, vcat, bmat, vmat, *, tq, meta_mode):
    j = pl.program_id(1)
    hd = HEAD_DIM

    kcat[0:WINDOW, :] = kkh_ref[...]
    kcat[WINDOW:, :] = kk_ref[...]
    vcat[0:WINDOW, :] = vvh_ref[...]
    vcat[WINDOW:, :] = vv_ref[...]

    def place(dst, par, src, rows, kvh, row0, nrows):
        lo_col = 0 if kvh == 0 else 2 * hd
        hi_col = 3 * hd if kvh == 0 else hd
        dst[par, kvh, row0:row0 + nrows, 0:hd] = src[rows, lo_col:lo_col + hd]
        dst[par, kvh, KCOLS + row0:KCOLS + row0 + nrows, hd:2 * hd] = src[rows, hi_col:hi_col + hd]

    bmat[...] = jnp.zeros_like(bmat)
    for par in range(2):
        for kvh in range(N_KV_HEADS):
            vmat[par, kvh, :, 0:LANES] = jnp.zeros((2 * KCOLS, LANES), BF16)
            vmat[par, kvh, 0:KCOLS, LANES:LANES + hd] = jnp.ones((KCOLS, hd), BF16)
            vmat[par, kvh, 0:KCOLS, LANES + hd:2 * LANES] = jnp.zeros((KCOLS, hd), BF16)
            vmat[par, kvh, KCOLS:2 * KCOLS, LANES:LANES + hd] = jnp.zeros((KCOLS, hd), BF16)
            vmat[par, kvh, KCOLS:2 * KCOLS, LANES + hd:2 * LANES] = jnp.ones((KCOLS, hd), BF16)
            place(bmat, par, kkm_ref, slice(None), kvh, KB, N_META)
            place(vmat, par, vvm_ref, slice(None), kvh, KB, N_META)

    ppk = N_HEADS // 2 // N_KV_HEADS

    def half_block(s, par):
        o = pl.multiple_of(s * QB, QB)
        if meta_mode:
            var = 3
            pos0 = jnp.asarray(s * QB - N_META, dtype=F32)
        else:
            first = jnp.logical_and(j == 0, s < 2)
            var = jnp.where(first, s + 1, 0)
            pos0 = (j * tq + s * QB).astype(F32)
        for kvh in range(N_KV_HEADS):
            place(bmat, par, kcat, pl.ds(o, KB), kvh, 0, KB)
            place(vmat, par, vcat, pl.ds(o, KB), kvh, 0, KB)
        for kvh in range(N_KV_HEADS):
            pairs = range(kvh * ppk, (kvh + 1) * ppk)
            qs = jnp.concatenate([q_ref[pl.ds(o, QB), p * LANES:(p + 1) * LANES] for p in pairs], axis=0)
            sc = lax.dot_general(qs, bmat[par, kvh], (((1,), (1,)), ((), ())), preferred_element_type=F32)
            probs = []
            for n, p in enumerate(pairs):
                sp = sc[n * QB:(n + 1) * QB, :] + (bias_ref[var, p] + mrow_ref[p] * pos0)
                m0 = jnp.max(sp[:, :KCOLS], axis=1, keepdims=True)
                m1 = jnp.max(sp[:, KCOLS:], axis=1, keepdims=True)
                pr = jnp.concatenate([jnp.exp2(sp[:, :KCOLS] - m0), jnp.exp2(sp[:, KCOLS:] - m1)], axis=1)
                probs.append(pr.astype(BF16))
            ov = jnp.dot(jnp.concatenate(probs, axis=0), vmat[par, kvh], preferred_element_type=F32)
            for n, p in enumerate(pairs):
                on = ov[n * QB:(n + 1) * QB, :]
                o_ref[pl.ds(o, QB), p * LANES:(p + 1) * LANES] = (on[:, :LANES] / on[:, LANES:]).astype(BF16)

    def two_half_blocks(s2, carry):
        half_block(2 * s2, 0)
        half_block(2 * s2 + 1, 1)
        return carry

    lax.fori_loop(0, tq // (2 * QB), two_half_blocks, 0)


def _attention(q, kk, vv, kkm, vvm, bias, mrow, *, batch, seq, tq, meta_mode):
    aw = N_HEADS * HEAD_DIM
    nj = seq // tq
    hb = tq // WINDOW if not meta_mode else 1

    def row_map(b, j):
        return (b * nj + j, 0)

    def halo_map(b, j):
        if meta_mode:
            return (0, 0)
        return (jnp.maximum(b * nj * hb + j * hb - 1, b * nj * hb), 0)

    const2 = lambda b, j: (0, 0)
    return pl.pallas_call(
        functools.partial(_attn_kernel, tq=tq, meta_mode=meta_mode),
        grid=(batch, nj),
        in_specs=[
            pl.BlockSpec((tq, aw), row_map),
            pl.BlockSpec((tq, 2 * KV_WIDTH), row_map),
            pl.BlockSpec((tq, 2 * KV_WIDTH), row_map),
            pl.BlockSpec((WINDOW, 2 * KV_WIDTH), halo_map),
            pl.BlockSpec((WINDOW, 2 * KV_WIDTH), halo_map),
            pl.BlockSpec((N_META, 2 * KV_WIDTH), const2),
            pl.BlockSpec((N_META, 2 * KV_WIDTH), const2),
            pl.BlockSpec(bias.shape, lambda b, j: (0, 0, 0, 0)),
            pl.BlockSpec(mrow.shape, lambda b, j: (0, 0, 0)),
        ],
        out_specs=pl.BlockSpec((tq, aw), row_map),
        out_shape=jax.ShapeDtypeStruct((batch * seq, aw), BF16),
        scratch_shapes=[
            pltpu.VMEM((WINDOW + tq, 2 * KV_WIDTH), BF16),
            pltpu.VMEM((WINDOW + tq, 2 * KV_WIDTH), BF16),
            pltpu.VMEM((2, N_KV_HEADS, 2 * KCOLS, LANES), BF16),
            pltpu.VMEM((2, N_KV_HEADS, 2 * KCOLS, 2 * LANES), BF16),
        ],
        compiler_params=_cparams(("parallel", "parallel")),
        name="attn_meta" if meta_mode else "attn",
    )(q, kk, vv, kk, vv, kkm, vvm, bias, mrow)


def _pool_matrices(meta):
    a = np.zeros((N_POOL_GROUPS, POOL_SUB, 2 * POOL_SUB), dtype=np.float32)
    for g, w in enumerate(POOL_WINDOWS):
        for t in range(POOL_SUB):
            lo = t - w + 1
            if meta:
                lo = max(lo, 0)
            cnt = t - lo + 1
            a[g, t, POOL_SUB + lo:POOL_SUB + t + 1] = 1.0 / cnt
            a[g, t, POOL_SUB + t] -= 1.0
    return a


def _merge_kernel(*refs, tm, with_router):
    (attn_ref, u_ref, uh_ref, um_ref, g_ref, h_ref, wa_ref, wp_ref, ps_ref, wo_ref, pa_ref,
     gain_ref) = refs[:12]
    refs = refs[12:]
    if with_router:
        rt_ref = refs[0]
        refs = refs[1:]
    hout_ref, hn_ref = refs[:2]
    refs = refs[2:]
    if with_router:
        ri_ref, rw_ref = refs[:2]
        refs = refs[2:]
    ucat, = refs

    j = pl.program_id(1)
    d = h_ref.shape[1]
    ucat[0:POOL_SUB, :] = jnp.where(j == 0, um_ref[...], uh_ref[...])
    ucat[POOL_SUB:, :] = u_ref[...]

    a = jnp.dot(attn_ref[...], wa_ref[...], preferred_element_type=F32)
    for r in range(tm // POOL_SUB):
        rows = slice(r * POOL_SUB, (r + 1) * POOL_SUB)
        parts = []
        for g in range(N_POOL_GROUPS):
            uw = ucat[r * POOL_SUB:(r + 2) * POOL_SUB, g * POOL_GROUP_DIM:(g + 1) * POOL_GROUP_DIM]
            pooled = jnp.dot(pa_ref[g], uw, preferred_element_type=F32).astype(BF16)
            parts.append(jnp.dot(pooled, wp_ref[g], preferred_element_type=F32))
        pb = jnp.concatenate(parts, axis=1) * ps_ref[...]
        gates = g_ref[rows, :]
        merged = gates[:, :d].astype(F32) * a[rows, :] + gates[:, d:].astype(F32) * pb
        out = jnp.dot(merged.astype(BF16), wo_ref[...], preferred_element_type=F32)
        hnew = h_ref[rows, :] + out
        hout_ref[rows, :] = hnew
        xn = _rms(hnew, gain_ref[...])
        xh = xn.astype(BF16)
        hn_ref[rows, :] = xn.astype(hn_ref.dtype)
        if with_router:
            xl = (xn - xh.astype(F32)).astype(BF16)
            nt = (((1,), (1,)), ((), ()))
            t_hi = lax.dot_general(rt_ref[...], xh, nt, preferred_element_type=F32)
            t_lo = lax.dot_general(rt_ref[...], xl, nt, preferred_element_type=F32)
            lg = t_hi[:N_EXPERTS] + t_hi[N_EXPERTS:] + t_lo[:N_EXPERTS]
            eid = lax.broadcasted_iota(jnp.int32, lg.shape, 0).astype(F32)
            m1 = jnp.max(lg, axis=0, keepdims=True)
            i1 = jnp.min(jnp.where(lg == m1, eid, float(N_EXPERTS)), axis=0, keepdims=True)
            lg2 = jnp.where(eid == i1, -jnp.inf, lg)
            m2 = jnp.max(lg2, axis=0, keepdims=True)
            i2 = jnp.min(jnp.where(lg2 == m2, eid, float(N_EXPERTS)), axis=0, keepdims=True)
            e2 = jnp.exp(m2 - m1)
            w1 = 1.0 / (1.0 + e2)
            w2 = e2 / (1.0 + e2)
            row = lax.broadcasted_iota(jnp.int32, lg.shape, 0)
            ri_ref[:, rows] = jnp.where(row == 0, i1, jnp.where(row == 1, i2, 0.0)).astype(jnp.int32)
            rw_ref[:, rows] = jnp.where(row == 0, w1, jnp.where(row == 1, w2, 0.0))


def _merge(attn, u, u_meta, gates, h, wa, wp, pscale, wo, pool_a, gain, rt, *, batch, seq, tm):
    t, d = h.shape
    nj = seq // tm
    hs = tm // POOL_SUB
    with_router = rt is not None

    def row_map(b, j):
        return (b * nj + j, 0)

    def halo_map(b, j):
        return (jnp.maximum((b * nj + j) * hs - 1, 0), 0)

    c2 = lambda b, j: (0, 0)
    c3 = lambda b, j: (0, 0, 0)
    in_specs = [
        pl.BlockSpec((tm, attn.shape[1]), row_map),
        pl.BlockSpec((tm, POOL_WIDTH), row_map),
        pl.BlockSpec((POOL_SUB, POOL_WIDTH), halo_map),
        pl.BlockSpec((POOL_SUB, POOL_WIDTH), c2),
        pl.BlockSpec((tm, 2 * d), row_map),
        pl.BlockSpec((tm, d), row_map),
        pl.BlockSpec(wa.shape, c2),
        pl.BlockSpec(wp.shape, c3),
        pl.BlockSpec((1, d), c2),
        pl.BlockSpec(wo.shape, c2),
        pl.BlockSpec(pool_a.shape, c3),
        pl.BlockSpec((1, d), c2),
    ]
    args = [attn, u, u, u_meta, gates, h, wa, wp, pscale, wo, pool_a, gain]
    out_specs = [pl.BlockSpec((tm, d), row_map), pl.BlockSpec((tm, d), row_map)]
    out_shape = [jax.ShapeDtypeStruct((t, d), F32), jax.ShapeDtypeStruct((t, d), BF16)]
    if with_router:
        in_specs.append(pl.BlockSpec(rt.shape, c2))
        args.append(rt)
        lane_map = lambda b, j: (0, b * nj + j)
        out_specs += [pl.BlockSpec((8, tm), lane_map), pl.BlockSpec((8, tm), lane_map)]
        out_shape += [jax.ShapeDtypeStruct((8, t), jnp.int32), jax.ShapeDtypeStruct((8, t), F32)]
    return pl.pallas_call(
        functools.partial(_merge_kernel, tm=tm, with_router=with_router),
        grid=(batch, nj),
        in_specs=in_specs,
        out_specs=out_specs,
        out_shape=out_shape,
        scratch_shapes=[pltpu.VMEM((POOL_SUB + tm, POOL_WIDTH), BF16)],
        compiler_params=_cparams(("parallel", "parallel")),
        name="merge_router" if with_router else "merge",
    )(*args)


def _swiglu_act(g, u):
    return (g * jax.nn.sigmoid(g)) * u


def _dense_ffn_kernel(hn_ref, h_ref, wg_ref, wu_ref, wd_ref, o_ref, *, fc):
    x = hn_ref[...]
    acc = h_ref[...]
    for c in range(wg_ref.shape[1] // fc):
        cols = slice(c * fc, (c + 1) * fc)
        g = jnp.dot(x, wg_ref[:, cols], preferred_element_type=F32)
        u = jnp.dot(x, wu_ref[:, cols], preferred_element_type=F32)
        acc = acc + jnp.dot(_swiglu_act(g, u).astype(BF16), wd_ref[cols, :], preferred_element_type=F32)
    o_ref[...] = acc


def _dense_ffn(hn, h, wg, wu, wd, tm, fc):
    t, d = h.shape
    row = pl.BlockSpec((tm, d), lambda i: (i, 0))
    c2 = lambda i: (0, 0)
    return pl.pallas_call(
        functools.partial(_dense_ffn_kernel, fc=fc),
        grid=(t // tm,),
        in_specs=[row, row, pl.BlockSpec(wg.shape, c2), pl.BlockSpec(wu.shape, c2), pl.BlockSpec(wd.shape, c2)],
        out_specs=row,
        out_shape=jax.ShapeDtypeStruct((t, d), F32),
        compiler_params=_cparams(("parallel",)),
        name="dense_ffn",
    )(hn, h, wg, wu, wd)


E_ROWS = 16


def _col_to_lanes(col, width):
    r = lax.broadcasted_iota(jnp.int32, (E_ROWS, width), 0)
    c = lax.broadcasted_iota(jnp.int32, (E_ROWS, width), 1)
    return jnp.sum(jnp.where(r == c, col, 0.0), axis=0, keepdims=True)


def _excl_cumsum_col(col):
    r = lax.broadcasted_iota(jnp.int32, (E_ROWS, E_ROWS), 0)
    c = lax.broadcasted_iota(jnp.int32, (E_ROWS, E_ROWS), 1)
    return jnp.sum(jnp.where(c < r, _col_to_lanes(col, E_ROWS), 0.0), axis=1, keepdims=True)


def _route_tab_kernel(ri_ref, rw_ref, tri_ref, slot_ref, tab_ref, fin_ref, carry, *, tn, tmg, nt_pad):
    i = pl.program_id(0)

    @pl.when(i == 0)
    def _():
        carry[...] = jnp.zeros_like(carry)

    eid = lax.broadcasted_iota(jnp.int32, (E_ROWS, tn), 0)
    ri = ri_ref[...]
    oh0 = (eid == ri[0:1, :]).astype(F32)
    oh1 = (eid == ri[1:2, :]).astype(F32)
    oh = oh0 + oh1
    cum = jnp.dot(oh.astype(BF16), tri_ref[...], preferred_element_type=F32)
    cnt = jnp.sum(oh, axis=1, keepdims=True)
    base = _excl_cumsum_col(cnt) + cum
    s0 = jnp.sum(oh0 * base, axis=0, keepdims=True)
    s1 = jnp.sum(oh1 * base, axis=0, keepdims=True)
    row = lax.broadcasted_iota(jnp.int32, (8, tn), 0)
    rw = rw_ref[...]
    slot_ref[...] = jnp.where(row == 0, s0, jnp.where(row == 1, s1, jnp.where(row == 2, rw[0:1, :],
                              jnp.where(row == 3, rw[1:2, :], 0.0))))
    trow = lax.broadcasted_iota(jnp.int32, (8, LANES), 0)
    tab_ref[...] = jnp.where(trow == 0, _col_to_lanes(cnt, LANES),
                             jnp.where(trow == 1, _col_to_lanes(carry[...], LANES), 0.0)).astype(jnp.int32)
    carry[...] += cnt

    @pl.when(i == pl.num_programs(0) - 1)
    def _():
        total = carry[...]
        padded = jnp.ceil(total / tmg) * tmg
        st = _excl_cumsum_col(padded)
        ends = st + padded
        tile0 = (lax.broadcasted_iota(jnp.int32, (E_ROWS, nt_pad), 1) * tmg).astype(F32)
        texp = jnp.minimum(jnp.sum((tile0 >= ends).astype(F32), axis=0, keepdims=True), N_EXPERTS - 1.0)
        nused = jnp.max(ends, axis=0, keepdims=True) / tmg
        frow = lax.broadcasted_iota(jnp.int32, (8, nt_pad), 0)
        fin_ref[...] = jnp.where(frow == 0, texp, jnp.where(frow == 1, nused, jnp.where(
            frow == 2, _col_to_lanes(st, nt_pad), jnp.where(frow == 3, _col_to_lanes(total, nt_pad), 0.0)))
        ).astype(jnp.int32)


def _route_tables(ri, rw, tn, tmg, nt_pad):
    t = ri.shape[1]
    nb = t // tn
    tri = jnp.asarray(np.triu(np.ones((tn, tn), dtype=np.float32), k=1), dtype=BF16)
    lane_blk = pl.BlockSpec((8, tn), lambda i: (0, i))
    return pl.pallas_call(
        functools.partial(_route_tab_kernel, tn=tn, tmg=tmg, nt_pad=nt_pad),
        grid=(nb,),
        in_specs=[lane_blk, lane_blk, pl.BlockSpec((tn, tn), lambda i: (0, 0))],
        out_specs=[lane_blk, pl.BlockSpec((None, 8, LANES), lambda i: (i, 0, 0)),
                   pl.BlockSpec((8, nt_pad), lambda i: (0, 0))],
        out_shape=[jax.ShapeDtypeStruct((8, t), F32), jax.ShapeDtypeStruct((nb, 8, LANES), jnp.int32),
                   jax.ShapeDtypeStruct((8, nt_pad), jnp.int32)],
        scratch_shapes=[pltpu.VMEM((E_ROWS, 1), F32)],
        compiler_params=_cparams(("arbitrary",)),
        name="route_tab",
    )(ri, rw, tri)


ROW_TILE = 8


def _pieces(n, max_log2, fn):
    for b in reversed(range(max_log2 + 1)):
        above = (n >> (b + 1)) << (b + 1)

        @pl.when(((n >> b) & 1) == 1)
        def _():
            fn(above, 1 << b)


def _tile_runs(i, ntab, basetab, gstart, tm, fn):
    off = jnp.int32(0)
    for e in range(N_EXPERTS):
        n_e = ntab[i * E_ROWS + e]
        dst = gstart[e] + basetab[i * E_ROWS + e]
        _pieces(n_e, tm.bit_length() - 1, lambda o, size, off=off, dst=dst: fn(off + o, dst + o, size))
        off = off + n_e


def _rows(ref, start, size):
    return ref.at[pl.ds(pl.multiple_of(start * ROW_TILE, ROW_TILE), size * ROW_TILE)]


def _scatter_kernel(ntab, basetab, gstart, gcount, x_ref, slot_ref, xs_hbm, sbuf, zbuf, sem, *, tm, tmg):
    i = pl.program_id(0)
    n_sorted = xs_hbm.shape[0] // ROW_TILE
    slots = slot_ref[...]
    rowid = lax.broadcasted_iota(jnp.int32, (2 * tm, tm), 0).astype(F32)
    perm = jnp.where(jnp.logical_or(rowid == slots[0:1, :], rowid == slots[1:2, :]), 1.0, 0.0).astype(BF16)
    srt = jnp.dot(perm, x_ref[...], preferred_element_type=F32)
    for c in range(ROW_TILE):
        sbuf[pl.ds(c, 2 * tm, stride=ROW_TILE), :] = srt[:, c * LANES:(c + 1) * LANES]

    def copy(src, dst, size):
        return pltpu.make_async_copy(_rows(sbuf, src, size), _rows(xs_hbm, dst, size), sem)

    _tile_runs(i, ntab, basetab, gstart, tm, lambda s, d, n: copy(s, d, n).start())
    _tile_runs(i, ntab, basetab, gstart, tm, lambda s, d, n: copy(s, d, n).wait())

    @pl.when(i == pl.num_programs(0) - 1)
    def _():
        zbuf[...] = jnp.zeros_like(zbuf)
        zrows = zbuf.shape[0] // ROW_TILE

        def zcopy(dst, size):
            return pltpu.make_async_copy(_rows(zbuf, 0, size), _rows(xs_hbm, dst, size), sem)

        used = gstart[N_EXPERTS]
        for act in ("start", "wait"):
            for e in range(N_EXPERTS):
                end = gstart[e] + gcount[e]
                npad = (tmg - (gcount[e] & (tmg - 1))) & (tmg - 1)
                _pieces(npad, zrows.bit_length() - 1,
                        lambda o, size, end=end, act=act: getattr(zcopy(end + o, size), act)())
            for k in range(N_EXPERTS * tmg // zrows):
                @pl.when(used + k * zrows < n_sorted)
                def _():
                    getattr(zcopy(used + k * zrows, zrows), act)()


def _scatter_rows(ntab, basetab, gstart, gcount, x, slotw, n_sorted, tm, tmg):
    t, d = x.shape
    assert d == ROW_TILE * LANES and tmg & (tmg - 1) == 0 and tm & (tm - 1) == 0
    grid_spec = pltpu.PrefetchScalarGridSpec(
        num_scalar_prefetch=4,
        grid=(t // tm,),
        in_specs=[pl.BlockSpec((tm, d), lambda i, *_: (i, 0)), pl.BlockSpec((8, tm), lambda i, *_: (0, i))],
        out_specs=pl.BlockSpec(memory_space=pl.ANY),
        scratch_shapes=[pltpu.VMEM((2 * tm * ROW_TILE, LANES), F32),
                        pltpu.VMEM((tmg // 2 * ROW_TILE, LANES), F32), pltpu.SemaphoreType.DMA],
    )
    return pl.pallas_call(
        functools.partial(_scatter_kernel, tm=tm, tmg=tmg),
        grid_spec=grid_spec,
        out_shape=jax.ShapeDtypeStruct((n_sorted * ROW_TILE, LANES), F32),
        compiler_params=_cparams(("arbitrary",)),
        name="moe_scatter",
    )(ntab, basetab, gstart, gcount, x, slotw)


def _grouped_ffn_kernel(te_ref, nu_ref, x_ref, wg_ref, wu_ref, wd_ref, y_ref, xb, acc):
    i = pl.program_id(0)
    j = pl.program_id(1)
    nj = pl.num_programs(1)

    @pl.when(i < nu_ref[0])
    def _():
        tmg = xb.shape[0]

        @pl.when(j == 0)
        def _():
            for c in range(ROW_TILE):
                xb[:, c * LANES:(c + 1) * LANES] = x_ref[pl.ds(c, tmg, stride=ROW_TILE), :].astype(BF16)
            acc[...] = jnp.zeros_like(acc)

        x = xb[...]
        g = jnp.dot(x, wg_ref[...], preferred_element_type=F32)
        u = jnp.dot(x, wu_ref[...], preferred_element_type=F32)
        acc[...] += jnp.dot(_swiglu_act(g, u).astype(BF16), wd_ref[...], preferred_element_type=F32)

        @pl.when(j == nj - 1)
        def _():
            for c in range(ROW_TILE):
                y_ref[pl.ds(c, tmg, stride=ROW_TILE), :] = acc[:, c * LANES:(c + 1) * LANES]

    @pl.when(jnp.logical_and(i >= nu_ref[0], j == nj - 1))
    def _():
        y_ref[...] = jnp.zeros_like(y_ref)


def _grouped_ffn(te, nused, xs, wg, wu, wd, tmg, fc, nt):
    d, f = wg.shape[1], wg.shape[2]
    nj = f // fc
    blk = tmg * ROW_TILE

    def tile(i, nu):
        return jnp.minimum(i, nu[0] - 1)

    def chunk(i, j, nu):
        return jnp.where(i < nu[0], j, nj - 1)

    grid_spec = pltpu.PrefetchScalarGridSpec(
        num_scalar_prefetch=2,
        grid=(nt, nj),
        in_specs=[
            pl.BlockSpec((blk, LANES), lambda i, j, te, nu: (tile(i, nu), 0)),
            pl.BlockSpec((None, d, fc), lambda i, j, te, nu: (te[tile(i, nu)], 0, chunk(i, j, nu))),
            pl.BlockSpec((None, d, fc), lambda i, j, te, nu: (te[tile(i, nu)], 0, chunk(i, j, nu))),
            pl.BlockSpec((None, fc, d), lambda i, j, te, nu: (te[tile(i, nu)], chunk(i, j, nu), 0)),
        ],
        out_specs=pl.BlockSpec((blk, LANES), lambda i, j, te, nu: (i, 0)),
        scratch_shapes=[pltpu.VMEM((tmg, d), BF16), pltpu.VMEM((tmg, d), F32)],
    )
    return pl.pallas_call(
        _grouped_ffn_kernel,
        grid_spec=grid_spec,
        out_shape=jax.ShapeDtypeStruct(xs.shape, F32),
        compiler_params=_cparams(("arbitrary", "arbitrary")),
        name="moe_ffn",
    )(te, nused, xs, wg, wu, wd)


def _combine_kernel(ntab, basetab, gstart, ys_hbm, slot_ref, h_ref, gain_ref, o_ref, ybuf, yg, sem, *, tm):
    i = pl.program_id(0)

    def copy(loc, src, size):
        return pltpu.make_async_copy(_rows(ys_hbm, src, size), _rows(ybuf, loc, size), sem)

    _tile_runs(i, ntab, basetab, gstart, tm, lambda s, d, n: copy(s, d, n).start())
    _tile_runs(i, ntab, basetab, gstart, tm, lambda s, d, n: copy(s, d, n).wait())
    for c in range(ROW_TILE):
        yg[:, c * LANES:(c + 1) * LANES] = ybuf[pl.ds(c, 2 * tm, stride=ROW_TILE), :].astype(BF16)

    sw = jnp.concatenate([slot_ref[...], jnp.zeros((LANES - 8, tm), F32)], axis=0).T
    lane = lax.broadcasted_iota(jnp.int32, (tm, 2 * tm), 1).astype(F32)
    y = h_ref[...]
    for k in range(2):
        pick = jnp.where(lane == sw[:, k:k + 1], 1.0, 0.0).astype(BF16)
        y = y + sw[:, 2 + k:3 + k] * jnp.dot(pick, yg[...], preferred_element_type=F32)
    o_ref[...] = _rms(y, gain_ref[...])


def _combine(ntab, basetab, gstart, ys, slotw, h, gain, tm):
    t, d = h.shape
    grid_spec = pltpu.PrefetchScalarGridSpec(
        num_scalar_prefetch=3,
        grid=(t // tm,),
        in_specs=[pl.BlockSpec(memory_space=pl.ANY),
                  pl.BlockSpec((8, tm), lambda i, *_: (0, i)),
                  pl.BlockSpec((tm, d), lambda i, *_: (i, 0)),
                  pl.BlockSpec((1, d), lambda i, *_: (0, 0))],
        out_specs=pl.BlockSpec((tm, d), lambda i, *_: (i, 0)),
        scratch_shapes=[pltpu.VMEM((2 * tm * ROW_TILE, LANES), F32), pltpu.VMEM((2 * tm, d), BF16),
                        pltpu.SemaphoreType.DMA],
    )
    return pl.pallas_call(
        functools.partial(_combine_kernel, tm=tm),
        grid_spec=grid_spec,
        out_shape=jax.ShapeDtypeStruct((t, d), F32),
        compiler_params=_cparams(("arbitrary",)),
        name="moe_combine",
    )(ntab, basetab, gstart, ys, slotw, h, gain)


def _half_if_aligned(n):
    return n // 2 if n % (2 * LANES) == 0 else n


def _tiles(seq, d_ff, d_ff_expert):
    return dict(
        tm_in=512,
        tq=min(1024, seq),
        tm_mg=512,
        tm_ffn=512, fc_ffn=_half_if_aligned(d_ff),
        tm_sc=512,
        tmg=512, fc_moe=_half_if_aligned(d_ff_expert),
    )


def kernel(x, meta_tokens, norm_mix, w_in, attn_sinks, w_attn_br, w_pool_grp, pool_scale, w_out, norm_ffn,
           dense_w_gate, dense_w_up, dense_w_down, moe_router, moe_w_gate, moe_w_up, moe_w_down, norm_final):
    batch, seq, d = x.shape
    depth = w_in.shape[0]
    assert depth == 2 and dense_w_gate.shape[0] == 1 and moe_router.shape[0] == 1, "dense layer then expert layer"
    assert moe_router.shape[2] == N_EXPERTS and w_in.shape[2] == N_HEADS * HEAD_DIM + 2 * KV_WIDTH + POOL_WIDTH + 2 * d
    t = batch * seq
    cfg = _tiles(seq, dense_w_gate.shape[2], moe_w_gate.shape[3])
    bf = lambda a: a.astype(BF16)

    bias_np, mrow_np, sinkpos_np = _attn_bias_tables()
    mrow = jnp.asarray(mrow_np)
    pool_a_main = jnp.asarray(_pool_matrices(False), dtype=BF16)
    pool_a_meta = jnp.asarray(_pool_matrices(True), dtype=BF16)

    h = x.reshape(t, d)
    hm = jnp.concatenate([meta_tokens.astype(F32), jnp.zeros((META_ROWS - N_META, d), F32)], axis=0)

    out = None
    for layer in range(depth):
        gain = norm_mix[layer].reshape(1, d)
        w_in_bf = bf(w_in[layer])
        sink_row = jnp.repeat(attn_sinks[layer].astype(F32) * LOG2E, KCOLS).reshape(N_HEADS // 2, 1, 2 * KCOLS)
        bias = jnp.asarray(bias_np) + (jnp.asarray(sinkpos_np) * sink_row)[None]
        wa, wo = bf(w_attn_br[layer]), bf(w_out[layer])
        wp = bf(w_pool_grp[layer])
        ps = pool_scale[layer].reshape(1, d)
        gain_ffn = norm_ffn[layer].reshape(1, d)

        qm, kkm_all, vvm_all, um, gm = _inproj(hm, gain, w_in_bf, META_ROWS)
        kkm, vvm = kkm_all[:N_META], vvm_all[:N_META]
        um16 = um[:N_META]
        q, kk, vv, u, g = _inproj(h, gain, w_in_bf, cfg["tm_in"])
        attn = _attention(q, kk, vv, kkm, vvm, bias, mrow, batch=batch, seq=seq, tq=cfg["tq"], meta_mode=False)

        u_before = jnp.concatenate([jnp.zeros((POOL_SUB - N_META, POOL_WIDTH), BF16), um16], axis=0)
        if layer == 0:
            attn_m = _attention(qm, kkm_all, vvm_all, kkm, vvm, bias, mrow,
                                batch=1, seq=META_ROWS, tq=META_ROWS, meta_mode=True)
            hm, hnm = _merge(attn_m, um, jnp.zeros_like(u_before), gm, hm, wa, wp, ps, wo, pool_a_meta, gain_ffn,
                             None, batch=1, seq=META_ROWS, tm=META_ROWS)
            h, hn = _merge(attn, u, u_before, g, h, wa, wp, ps, wo, pool_a_main, gain_ffn, None,
                           batch=batch, seq=seq, tm=cfg["tm_mg"])
            wg, wu, wd = bf(dense_w_gate[0]), bf(dense_w_up[0]), bf(dense_w_down[0])
            hm = _dense_ffn(hnm, hm, wg, wu, wd, META_ROWS, cfg["fc_ffn"])
            h = _dense_ffn(hn, h, wg, wu, wd, cfg["tm_ffn"], cfg["fc_ffn"])
        else:
            r = moe_router[0].astype(F32)
            r_hi = r.astype(BF16)
            r_lo = (r - r_hi.astype(F32)).astype(BF16)
            rt = jnp.concatenate([r_hi.T, r_lo.T], axis=0)
            h, hn, ri, rw = _merge(attn, u, u_before, g, h, wa, wp, ps, wo, pool_a_main, gain_ffn, rt,
                                   batch=batch, seq=seq, tm=cfg["tm_mg"])
            wg, wu, wd = bf(moe_w_gate[0]), bf(moe_w_up[0]), bf(moe_w_down[0])
            out = _moe(h, hn, ri, rw, wg, wu, wd, norm_final.reshape(1, d), cfg)
    return out.reshape(batch, seq, d)


def _moe(h, hn, ri, rw, wg, wu, wd, gain_final, cfg):
    t, d = h.shape
    tmg, tm = cfg["tmg"], cfg["tm_sc"]
    nt = 2 * t // tmg + N_EXPERTS
    nt_pad = -(-nt // LANES) * LANES
    slotw, tab, fin = _route_tables(ri, rw, tm, tmg, nt_pad)
    ntab = tab[:, 0, :E_ROWS].reshape(-1)
    basetab = tab[:, 1, :E_ROWS].reshape(-1)
    te, nused, gstart, gcount = fin[0, :nt], fin[1, :1], fin[2, :E_ROWS], fin[3, :E_ROWS]
    xs = _scatter_rows(ntab, basetab, gstart, gcount, hn, slotw, nt * tmg, tm, tmg)
    ys = _grouped_ffn(te, nused, xs, wg, wu, wd, tmg, cfg["fc_moe"], nt)
    return _combine(ntab, basetab, gstart, ys, slotw, h, gain_final, tm)
```

```python
import functools
import math

import numpy as np
import jax
import jax.numpy as jnp
from jax import lax
from jax.experimental import pallas as pl
from jax.experimental.pallas import tpu as pltpu

F32 = jnp.float32
BF16 = jnp.bfloat16

N_HEADS = 16
HEAD_DIM = 64
N_KV_HEADS = 2
KV_WIDTH = N_KV_HEADS * HEAD_DIM
WINDOW = 128
N_META = 16
POOL_WINDOWS = (2, 4, 8, 16)
N_POOL_GROUPS = 4
POOL_GROUP_DIM = 128
POOL_WIDTH = N_POOL_GROUPS * POOL_GROUP_DIM
N_EXPERTS = 8
RMS_EPS = 1e-5
NEG_BIAS = -1e30
LOG2E = 1.4426950408889634

LANES = 128
META_ROWS = 128
VMEM_LIMIT = 56 * 1024 * 1024

QB = 64
KB = WINDOW + QB
KCOLS = 256
SINK_COL = KB + N_META


def _cparams(sem, vmem=VMEM_LIMIT):
    return pltpu.CompilerParams(dimension_semantics=sem, vmem_limit_bytes=vmem)


def _rms(x, gain):
    ms = jnp.mean(x * x, axis=-1, keepdims=True)
    return x * lax.rsqrt(ms + RMS_EPS) * gain


def _inproj_kernel(h_ref, gain_ref, w_ref, q_ref, kk_ref, vv_ref, u_ref, g_ref, *, d_model):
    xn = _rms(h_ref[...], gain_ref[...]).astype(BF16)
    aw = N_HEADS * HEAD_DIM
    qscale = (HEAD_DIM ** -0.5) * LOG2E
    half = aw // 2
    for c in range(2):
        q = jnp.dot(xn, w_ref[:, c * half:(c + 1) * half], preferred_element_type=F32)
        q_ref[:, c * half:(c + 1) * half] = (q * qscale).astype(BF16)
    kv = jnp.dot(xn, w_ref[:, aw:aw + 2 * KV_WIDTH], preferred_element_type=F32)
    k = kv[:, :KV_WIDTH]
    v = kv[:, KV_WIDTH:]
    kk_ref[:, :KV_WIDTH] = k.astype(BF16)
    kk_ref[:, KV_WIDTH:] = pltpu.roll(k, HEAD_DIM, 1).astype(BF16)
    vv_ref[:, :KV_WIDTH] = v.astype(BF16)
    vv_ref[:, KV_WIDTH:] = pltpu.roll(v, HEAD_DIM, 1).astype(BF16)
    o = aw + 2 * KV_WIDTH
    u_ref[...] = jnp.dot(xn, w_ref[:, o:o + POOL_WIDTH], preferred_element_type=F32).astype(BF16)
    o += POOL_WIDTH
    gw = 2 * d_model
    gc = 512
    for c in range(gw // gc):
        g = jnp.dot(xn, w_ref[:, o + c * gc:o + (c + 1) * gc], preferred_element_type=F32)
        g_ref[:, c * gc:(c + 1) * gc] = jax.nn.sigmoid(g).astype(BF16)


def _inproj(h, gain, w_bf, tm):
    t, d = h.shape
    n = w_bf.shape[1]
    aw = N_HEADS * HEAD_DIM
    row = lambda w: pl.BlockSpec((tm, w), lambda i: (i, 0))
    return pl.pallas_call(
        functools.partial(_inproj_kernel, d_model=d),
        grid=(t // tm,),
        in_specs=[row(d), pl.BlockSpec((1, d), lambda i: (0, 0)), pl.BlockSpec((d, n), lambda i: (0, 0))],
        out_specs=[row(aw), row(2 * KV_WIDTH), row(2 * KV_WIDTH), row(POOL_WIDTH), row(2 * d)],
        out_shape=[
            jax.ShapeDtypeStruct((t, aw), BF16),
            jax.ShapeDtypeStruct((t, 2 * KV_WIDTH), BF16),
            jax.ShapeDtypeStruct((t, 2 * KV_WIDTH), BF16),
            jax.ShapeDtypeStruct((t, POOL_WIDTH), BF16),
            jax.ShapeDtypeStruct((t, 2 * d), BF16),
        ],
        compiler_params=_cparams(("parallel",)),
        name="inproj",
    )(h, gain, w_bf)


def _attn_bias_tables():
    slopes = np.array([2.0 ** (-8.0 * (h + 1) / N_HEADS) for h in range(N_HEADS)], dtype=np.float64)
    i = np.arange(QB)[:, None]
    c = np.arange(KCOLS)[None, :]
    d_band = WINDOW + i - c
    ok_band = (c < KB) & (d_band >= 0) & (d_band < WINDOW)
    m = c - KB
    is_meta = (c >= KB) & (c < KB + N_META)
    d_meta = N_META + i - m
    tbl = np.full((4, N_HEADS, QB, KCOLS), NEG_BIAS, dtype=np.float64)
    for var in range(4):
        if var == 0:
            okb = ok_band
        elif var == 1:
            okb = ok_band & (c >= WINDOW)
        elif var == 2:
            okb = ok_band & (c >= WINDOW - QB)
        else:
            okb = np.zeros_like(ok_band)
        okm = np.broadcast_to(is_meta, (QB, KCOLS)) & ((d_meta - N_META >= 0) if var == 3 else True)
        okb = np.broadcast_to(okb, (QB, KCOLS))
        for h in range(N_HEADS):
            t = tbl[var, h]
            t[okb] = (-slopes[h] * LOG2E * d_band)[okb]
            t[okm] = (-slopes[h] * LOG2E * d_meta)[okm]
            t[:, SINK_COL] = 0.0
    tbl = tbl.reshape(4, N_HEADS // 2, 2, QB, KCOLS).transpose(0, 1, 3, 2, 4)
    tbl = tbl.reshape(4, N_HEADS // 2, QB, 2 * KCOLS)
    mrow = np.zeros((N_HEADS, KCOLS), dtype=np.float64)
    mrow[:, KB:KB + N_META] = (-slopes * LOG2E)[:, None]
    mrow = mrow.reshape(N_HEADS // 2, 1, 2 * KCOLS)
    sink_pos = np.zeros((N_HEADS // 2, 1, 2 * KCOLS), dtype=np.float32)
    sink_pos[:, 0, SINK_COL] = 1.0
    sink_pos[:, 0, KCOLS + SINK_COL] = 1.0
    return tbl.astype(np.float32), mrow.astype(np.float32), sink_pos


def _attn_kernel(q_ref, kk_ref, vv_ref, kkh_ref, vvh_ref, kkm_ref, vvm_ref, bias_ref, mrow_ref,
                 o_ref, kcat, vcat, bmat, vmat, *, tq, meta_mode):
    hb_per_iter = bmat.shape[0]
    j = pl.program_id(1)
    hd = HEAD_DIM

    kcat[0:WINDOW, :] = kkh_ref[...]
    kcat[WINDOW:, :] = kk_ref[...]
    vcat[0:WINDOW, :] = vvh_ref[...]
    vcat[WINDOW:, :] = vv_ref[...]

    def place(dst, par, src, rows, kvh, row0, nrows):
        lo_col = 0 if kvh == 0 else 2 * hd
        hi_col = 3 * hd if kvh == 0 else hd
        dst[par, kvh, row0:row0 + nrows, 0:hd] = src[rows, lo_col:lo_col + hd]
        dst[par, kvh, KCOLS + row0:KCOLS + row0 + nrows, hd:2 * hd] = src[rows, hi_col:hi_col + hd]

    @pl.when(jnp.logical_and(pl.program_id(0) == 0, j == 0))
    def _():
        bmat[...] = jnp.zeros_like(bmat)
        for par in range(hb_per_iter):
            for kvh in range(N_KV_HEADS):
                vmat[par, kvh, :, 0:LANES] = jnp.zeros((2 * KCOLS, LANES), BF16)
                vmat[par, kvh, 0:KCOLS, LANES:LANES + hd] = jnp.ones((KCOLS, hd), BF16)
                vmat[par, kvh, 0:KCOLS, LANES + hd:2 * LANES] = jnp.zeros((KCOLS, hd), BF16)
                vmat[par, kvh, KCOLS:2 * KCOLS, LANES:LANES + hd] = jnp.zeros((KCOLS, hd), BF16)
                vmat[par, kvh, KCOLS:2 * KCOLS, LANES + hd:2 * LANES] = jnp.ones((KCOLS, hd), BF16)
                place(bmat, par, kkm_ref, slice(None), kvh, KB, N_META)
                place(vmat, par, vvm_ref, slice(None), kvh, KB, N_META)

    ppk = N_HEADS // 2 // N_KV_HEADS

    def half_block(s, par):
        o = pl.multiple_of(s * QB, QB)
        if meta_mode:
            var = 3
            pos0 = jnp.asarray(s * QB - N_META, dtype=F32)
        else:
            first = jnp.logical_and(j == 0, s < 2)
            var = jnp.where(first, s + 1, 0)
            pos0 = (j * tq + s * QB).astype(F32)
        for kvh in range(N_KV_HEADS):
            place(bmat, par, kcat, pl.ds(o, KB), kvh, 0, KB)
            place(vmat, par, vcat, pl.ds(o, KB), kvh, 0, KB)
        for kvh in range(N_KV_HEADS):
            pairs = range(kvh * ppk, (kvh + 1) * ppk)
            qs = jnp.concatenate([q_ref[pl.ds(o, QB), p * LANES:(p + 1) * LANES] for p in pairs], axis=0)
            sc = lax.dot_general(qs, bmat[par, kvh], (((1,), (1,)), ((), ())), preferred_element_type=F32)
            probs = []
            for n, p in enumerate(pairs):
                sp = sc[n * QB:(n + 1) * QB, :] + (bias_ref[var, p] + mrow_ref[p] * pos0)
                m0 = jnp.max(sp[:, :KCOLS], axis=1, keepdims=True)
                m1 = jnp.max(sp[:, KCOLS:], axis=1, keepdims=True)
                pr = jnp.concatenate([jnp.exp2(sp[:, :KCOLS] - m0), jnp.exp2(sp[:, KCOLS:] - m1)], axis=1)
                probs.append(pr.astype(BF16))
            ov = jnp.dot(jnp.concatenate(probs, axis=0), vmat[par, kvh], preferred_element_type=F32)
            for n, p in enumerate(pairs):
                on = ov[n * QB:(n + 1) * QB, :]
                o_ref[pl.ds(o, QB), p * LANES:(p + 1) * LANES] = (on[:, :LANES] / on[:, LANES:]).astype(BF16)

    def group(sg, carry):
        for n in range(hb_per_iter):
            half_block(hb_per_iter * sg + n, n)
        return carry

    lax.fori_loop(0, tq // (hb_per_iter * QB), group, 0)


def _attention(q, kk, vv, kkm, vvm, bias, mrow, *, batch, seq, tq, meta_mode):
    aw = N_HEADS * HEAD_DIM
    nj = seq // tq
    hb = tq // WINDOW if not meta_mode else 1
    hb_per_iter = min(4, tq // QB)

    def row_map(b, j):
        return (b * nj + j, 0)

    def halo_map(b, j):
        if meta_mode:
            return (0, 0)
        return (jnp.maximum(b * nj * hb + j * hb - 1, b * nj * hb), 0)

    const2 = lambda b, j: (0, 0)
    return pl.pallas_call(
        functools.partial(_attn_kernel, tq=tq, meta_mode=meta_mode),
        grid=(batch, nj),
        in_specs=[
            pl.BlockSpec((tq, aw), row_map),
            pl.BlockSpec((tq, 2 * KV_WIDTH), row_map),
            pl.BlockSpec((tq, 2 * KV_WIDTH), row_map),
            pl.BlockSpec((WINDOW, 2 * KV_WIDTH), halo_map),
            pl.BlockSpec((WINDOW, 2 * KV_WIDTH), halo_map),
            pl.BlockSpec((N_META, 2 * KV_WIDTH), const2),
            pl.BlockSpec((N_META, 2 * KV_WIDTH), const2),
            pl.BlockSpec(bias.shape, lambda b, j: (0, 0, 0, 0)),
            pl.BlockSpec(mrow.shape, lambda b, j: (0, 0, 0)),
        ],
        out_specs=pl.BlockSpec((tq, aw), row_map),
        out_shape=jax.ShapeDtypeStruct((batch * seq, aw), BF16),
        scratch_shapes=[
            pltpu.VMEM((WINDOW + tq, 2 * KV_WIDTH), BF16),
            pltpu.VMEM((WINDOW + tq, 2 * KV_WIDTH), BF16),
            pltpu.VMEM((hb_per_iter, N_KV_HEADS, 2 * KCOLS, LANES), BF16),
            pltpu.VMEM((hb_per_iter, N_KV_HEADS, 2 * KCOLS, 2 * LANES), BF16),
        ],
        compiler_params=_cparams(("arbitrary", "arbitrary")),
        name="attn_meta" if meta_mode else "attn",
    )(q, kk, vv, kk, vv, kkm, vvm, bias, mrow)


POOL_HALO = 16


def _pool_inverse_counts(tm, clip_at_row0):
    t = np.arange(tm, dtype=np.float64)[None, :, None]
    w = np.asarray(POOL_WINDOWS, dtype=np.float64)[:, None, None]
    cnt = np.minimum(t + 1, w) if clip_at_row0 else np.broadcast_to(w, (N_POOL_GROUPS, tm, 1))
    return (1.0 / cnt).astype(np.float32)


def _merge_kernel(*refs, tm, with_router):
    (attn_ref, u_ref, uh_ref, um_ref, g_ref, h_ref, wa_ref, wp_ref, ps_ref, wo_ref, pinv_ref,
     gain_ref) = refs[:12]
    refs = refs[12:]
    if with_router:
        rt_ref = refs[0]
        refs = refs[1:]
    hout_ref, hn_ref = refs[:2]
    refs = refs[2:]
    if with_router:
        ri_ref, rw_ref = refs[:2]

    j = pl.program_id(1)
    d = h_ref.shape[1]
    a = jnp.dot(attn_ref[...], wa_ref[...], preferred_element_type=F32)

    before = jnp.where(j == 0, um_ref[...], uh_ref[...])
    x = jnp.concatenate([before, u_ref[...]], axis=0).astype(F32)
    parts = []
    for g, w in enumerate(POOL_WINDOWS):
        xg = x[:, g * POOL_GROUP_DIM:(g + 1) * POOL_GROUP_DIM]
        s, k = xg, 1
        while k < w:
            s = s + pltpu.roll(s, k, 0)
            k *= 2
        pooled = s[POOL_HALO:] * pinv_ref[g] - xg[POOL_HALO:]
        parts.append(jnp.dot(pooled.astype(BF16), wp_ref[g], preferred_element_type=F32))
    pb = jnp.concatenate(parts, axis=1) * ps_ref[...]
    gates = g_ref[...]
    merged = gates[:, :d].astype(F32) * a + gates[:, d:].astype(F32) * pb
    out = jnp.dot(merged.astype(BF16), wo_ref[...], preferred_element_type=F32)
    hnew = h_ref[...] + out
    hout_ref[...] = hnew
    xn = _rms(hnew, gain_ref[...])
    xh = xn.astype(BF16)
    hn_ref[...] = xh
    if with_router:
        xl = (xn - xh.astype(F32)).astype(BF16)
        nt = (((1,), (1,)), ((), ()))
        t_hi = lax.dot_general(rt_ref[...], xh, nt, preferred_element_type=F32)
        t_lo = lax.dot_general(rt_ref[...], xl, nt, preferred_element_type=F32)
        lg = t_hi[:N_EXPERTS] + t_hi[N_EXPERTS:] + t_lo[:N_EXPERTS]
        eid = lax.broadcasted_iota(jnp.int32, lg.shape, 0).astype(F32)
        m1 = jnp.max(lg, axis=0, keepdims=True)
        i1 = jnp.min(jnp.where(lg == m1, eid, float(N_EXPERTS)), axis=0, keepdims=True)
        lg2 = jnp.where(eid == i1, -jnp.inf, lg)
        m2 = jnp.max(lg2, axis=0, keepdims=True)
        i2 = jnp.min(jnp.where(lg2 == m2, eid, float(N_EXPERTS)), axis=0, keepdims=True)
        e2 = jnp.exp(m2 - m1)
        w1 = 1.0 / (1.0 + e2)
        w2 = e2 / (1.0 + e2)
        row = lax.broadcasted_iota(jnp.int32, lg.shape, 0)
        ri_ref[...] = jnp.where(row == 0, i1, jnp.where(row == 1, i2, 0.0)).astype(jnp.int32)
        rw_ref[...] = jnp.where(row == 0, w1, jnp.where(row == 1, w2, 0.0))


def _merge(attn, u, u_meta, gates, h, wa, wp, pscale, wo, pool_inv, gain, rt, *, batch, seq, tm):
    t, d = h.shape
    nj = seq // tm
    hs = tm // POOL_HALO
    with_router = rt is not None

    def row_map(b, j):
        return (b * nj + j, 0)

    def halo_map(b, j):
        return (jnp.maximum((b * nj + j) * hs - 1, 0), 0)

    c2 = lambda b, j: (0, 0)
    c3 = lambda b, j: (0, 0, 0)
    in_specs = [
        pl.BlockSpec((tm, attn.shape[1]), row_map),
        pl.BlockSpec((tm, POOL_WIDTH), row_map),
        pl.BlockSpec((POOL_HALO, POOL_WIDTH), halo_map),
        pl.BlockSpec((POOL_HALO, POOL_WIDTH), c2),
        pl.BlockSpec((tm, 2 * d), row_map),
        pl.BlockSpec((tm, d), row_map),
        pl.BlockSpec(wa.shape, c2),
        pl.BlockSpec(wp.shape, c3),
        pl.BlockSpec((1, d), c2),
        pl.BlockSpec(wo.shape, c2),
        pl.BlockSpec(pool_inv.shape, c3),
        pl.BlockSpec((1, d), c2),
    ]
    args = [attn, u, u, u_meta, gates, h, wa, wp, pscale, wo, pool_inv, gain]
    out_specs = [pl.BlockSpec((tm, d), row_map), pl.BlockSpec((tm, d), row_map)]
    out_shape = [jax.ShapeDtypeStruct((t, d), F32), jax.ShapeDtypeStruct((t, d), BF16)]
    if with_router:
        in_specs.append(pl.BlockSpec(rt.shape, c2))
        args.append(rt)
        lane_map = lambda b, j: (0, b * nj + j)
        out_specs += [pl.BlockSpec((8, tm), lane_map), pl.BlockSpec((8, tm), lane_map)]
        out_shape += [jax.ShapeDtypeStruct((8, t), jnp.int32), jax.ShapeDtypeStruct((8, t), F32)]
    return pl.pallas_call(
        functools.partial(_merge_kernel, tm=tm, with_router=with_router),
        grid=(batch, nj),
        in_specs=in_specs,
        out_specs=out_specs,
        out_shape=out_shape,
        compiler_params=_cparams(("parallel", "parallel")),
        name="merge_router" if with_router else "merge",
    )(*args)


def _swiglu_act(g, u):
    return (g * jax.nn.sigmoid(g)) * u


def _dense_ffn_kernel(hn_ref, h_ref, wg_ref, wu_ref, wd_ref, o_ref, *, fc):
    x = hn_ref[...]
    acc = h_ref[...]
    for c in range(wg_ref.shape[1] // fc):
        cols = slice(c * fc, (c + 1) * fc)
        g = jnp.dot(x, wg_ref[:, cols], preferred_element_type=F32)
        u = jnp.dot(x, wu_ref[:, cols], preferred_element_type=F32)
        acc = acc + jnp.dot(_swiglu_act(g, u).astype(BF16), wd_ref[cols, :], preferred_element_type=F32)
    o_ref[...] = acc


def _dense_ffn(hn, h, wg, wu, wd, tm, fc):
    t, d = h.shape
    row = pl.BlockSpec((tm, d), lambda i: (i, 0))
    c2 = lambda i: (0, 0)
    return pl.pallas_call(
        functools.partial(_dense_ffn_kernel, fc=fc),
        grid=(t // tm,),
        in_specs=[row, row, pl.BlockSpec(wg.shape, c2), pl.BlockSpec(wu.shape, c2), pl.BlockSpec(wd.shape, c2)],
        out_specs=row,
        out_shape=jax.ShapeDtypeStruct((t, d), F32),
        compiler_params=_cparams(("parallel",)),
        name="dense_ffn",
    )(hn, h, wg, wu, wd)


E_ROWS = 16


def _col_to_lanes(col, width):
    r = lax.broadcasted_iota(jnp.int32, (E_ROWS, width), 0)
    c = lax.broadcasted_iota(jnp.int32, (E_ROWS, width), 1)
    return jnp.sum(jnp.where(r == c, col, 0.0), axis=0, keepdims=True)


def _excl_cumsum_col(col):
    r = lax.broadcasted_iota(jnp.int32, (E_ROWS, E_ROWS), 0)
    c = lax.broadcasted_iota(jnp.int32, (E_ROWS, E_ROWS), 1)
    return jnp.sum(jnp.where(c < r, _col_to_lanes(col, E_ROWS), 0.0), axis=1, keepdims=True)


def _route_tab_kernel(ri_ref, rw_ref, tri_ref, slot_ref, tab_ref, fin_ref, carry, *, tn, tmg, nt_pad):
    i = pl.program_id(0)

    @pl.when(i == 0)
    def _():
        carry[...] = jnp.zeros_like(carry)

    eid = lax.broadcasted_iota(jnp.int32, (E_ROWS, tn), 0)
    ri = ri_ref[...]
    oh0 = (eid == ri[0:1, :]).astype(F32)
    oh1 = (eid == ri[1:2, :]).astype(F32)
    oh = oh0 + oh1
    cum = jnp.dot(oh.astype(BF16), tri_ref[...], preferred_element_type=F32)
    cnt = jnp.sum(oh, axis=1, keepdims=True)
    base = _excl_cumsum_col(cnt) + cum
    s0 = jnp.sum(oh0 * base, axis=0, keepdims=True)
    s1 = jnp.sum(oh1 * base, axis=0, keepdims=True)
    row = lax.broadcasted_iota(jnp.int32, (8, tn), 0)
    rw = rw_ref[...]
    slot_ref[...] = jnp.where(row == 0, s0, jnp.where(row == 1, s1, jnp.where(row == 2, rw[0:1, :],
                              jnp.where(row == 3, rw[1:2, :], 0.0))))
    trow = lax.broadcasted_iota(jnp.int32, (8, LANES), 0)
    tab_ref[...] = jnp.where(trow == 0, _col_to_lanes(cnt, LANES),
                             jnp.where(trow == 1, _col_to_lanes(carry[...], LANES), 0.0)).astype(jnp.int32)
    carry[...] += cnt

    @pl.when(i == pl.num_programs(0) - 1)
    def _():
        total = carry[...]
        padded = jnp.ceil(total / tmg) * tmg
        st = _excl_cumsum_col(padded)
        ends = st + padded
        tile0 = (lax.broadcasted_iota(jnp.int32, (E_ROWS, nt_pad), 1) * tmg).astype(F32)
        texp = jnp.minimum(jnp.sum((tile0 >= ends).astype(F32), axis=0, keepdims=True), N_EXPERTS - 1.0)
        nused = jnp.max(ends, axis=0, keepdims=True) / tmg
        frow = lax.broadcasted_iota(jnp.int32, (8, nt_pad), 0)
        fin_ref[...] = jnp.where(frow == 0, texp, jnp.where(frow == 1, nused, jnp.where(
            frow == 2, _col_to_lanes(st, nt_pad), jnp.where(frow == 3, _col_to_lanes(total, nt_pad), 0.0)))
        ).astype(jnp.int32)


def _route_tables(ri, rw, tn, tmg, nt_pad):
    t = ri.shape[1]
    nb = t // tn
    tri = jnp.asarray(np.triu(np.ones((tn, tn), dtype=np.float32), k=1), dtype=BF16)
    lane_blk = pl.BlockSpec((8, tn), lambda i: (0, i))
    return pl.pallas_call(
        functools.partial(_route_tab_kernel, tn=tn, tmg=tmg, nt_pad=nt_pad),
        grid=(nb,),
        in_specs=[lane_blk, lane_blk, pl.BlockSpec((tn, tn), lambda i: (0, 0))],
        out_specs=[lane_blk, pl.BlockSpec((None, 8, LANES), lambda i: (i, 0, 0)),
                   pl.BlockSpec((8, nt_pad), lambda i: (0, 0))],
        out_shape=[jax.ShapeDtypeStruct((8, t), F32), jax.ShapeDtypeStruct((nb, 8, LANES), jnp.int32),
                   jax.ShapeDtypeStruct((8, nt_pad), jnp.int32)],
        scratch_shapes=[pltpu.VMEM((E_ROWS, 1), F32)],
        compiler_params=_cparams(("arbitrary",)),
        name="route_tab",
    )(ri, rw, tri)


ROW_TILE = 8


def _pieces(n, max_log2, fn):
    for b in reversed(range(max_log2 + 1)):
        above = (n >> (b + 1)) << (b + 1)

        @pl.when(((n >> b) & 1) == 1)
        def _():
            fn(above, 1 << b)


def _tile_runs(i, ntab, basetab, gstart, tm, fn):
    off = jnp.int32(0)
    for e in range(N_EXPERTS):
        n_e = ntab[i * E_ROWS + e]
        dst = gstart[e] + basetab[i * E_ROWS + e]
        _pieces(n_e, tm.bit_length() - 1, lambda o, size, off=off, dst=dst: fn(off + o, dst + o, size))
        off = off + n_e


def _rows(ref, start, size):
    return ref.at[pl.ds(pl.multiple_of(start * ROW_TILE, ROW_TILE), size * ROW_TILE)]


def _scatter_kernel(ntab, basetab, gstart, gcount, x_ref, slot_ref, xs_hbm, sbuf, zbuf, sem, zsem, *, tm, tmg):
    i = pl.program_id(0)
    last = pl.num_programs(0) - 1
    n_sorted = xs_hbm.shape[0] // ROW_TILE
    half = lax.rem(i, 2)
    slots = slot_ref[...]
    rowid = lax.broadcasted_iota(jnp.int32, (2 * tm, tm), 0).astype(F32)
    perm = jnp.where(jnp.logical_or(rowid == slots[0:1, :], rowid == slots[1:2, :]), 1.0, 0.0).astype(BF16)
    srt = jnp.dot(perm, x_ref[...], preferred_element_type=F32)
    sub0 = pl.multiple_of(half * (2 * tm * ROW_TILE), ROW_TILE)
    for c in range(ROW_TILE):
        sbuf[pl.ds(sub0 + c, 2 * tm, stride=ROW_TILE), :] = srt[:, c * LANES:(c + 1) * LANES]

    def runs(tile, which, act):
        def go(src, dst, size):
            cp = pltpu.make_async_copy(_rows(sbuf, which * (2 * tm) + src, size), _rows(xs_hbm, dst, size),
                                       sem.at[which])
            getattr(cp, act)()
        _tile_runs(tile, ntab, basetab, gstart, tm, go)

    @pl.when(i > 0)
    def _():
        runs(i - 1, 1 - half, "wait")

    runs(i, half, "start")

    @pl.when(i == last)
    def _():
        runs(i, half, "wait")
        zbuf[...] = jnp.zeros_like(zbuf)
        zrows = zbuf.shape[0] // ROW_TILE

        def zcopy(dst, size):
            return pltpu.make_async_copy(_rows(zbuf, 0, size), _rows(xs_hbm, dst, size), zsem)

        used = gstart[N_EXPERTS]
        for act in ("start", "wait"):
            for e in range(N_EXPERTS):
                end = gstart[e] + gcount[e]
                npad = (tmg - (gcount[e] & (tmg - 1))) & (tmg - 1)
                _pieces(npad, zrows.bit_length() - 1,
                        lambda o, size, end=end, act=act: getattr(zcopy(end + o, size), act)())
            for k in range(N_EXPERTS * tmg // zrows):
                @pl.when(used + k * zrows < n_sorted)
                def _():
                    getattr(zcopy(used + k * zrows, zrows), act)()


def _scatter_rows(ntab, basetab, gstart, gcount, x, slotw, n_sorted, tm, tmg):
    t, d = x.shape
    assert d == ROW_TILE * LANES and tmg & (tmg - 1) == 0 and tm & (tm - 1) == 0
    grid_spec = pltpu.PrefetchScalarGridSpec(
        num_scalar_prefetch=4,
        grid=(t // tm,),
        in_specs=[pl.BlockSpec((tm, d), lambda i, *_: (i, 0)), pl.BlockSpec((8, tm), lambda i, *_: (0, i))],
        out_specs=pl.BlockSpec(memory_space=pl.ANY),
        scratch_shapes=[pltpu.VMEM((2 * 2 * tm * ROW_TILE, LANES), F32),
                        pltpu.VMEM((tmg // 2 * ROW_TILE, LANES), F32),
                        pltpu.SemaphoreType.DMA((2,)), pltpu.SemaphoreType.DMA],
    )
    return pl.pallas_call(
        functools.partial(_scatter_kernel, tm=tm, tmg=tmg),
        grid_spec=grid_spec,
        out_shape=jax.ShapeDtypeStruct((n_sorted * ROW_TILE, LANES), F32),
        compiler_params=_cparams(("arbitrary",)),
        name="moe_scatter",
    )(ntab, basetab, gstart, gcount, x, slotw)


def _grouped_ffn_kernel(te_ref, nu_ref, x_ref, wg_ref, wu_ref, wd_ref, y_ref, xb, acc):
    i = pl.program_id(0)
    j = pl.program_id(1)
    nj = pl.num_programs(1)

    @pl.when(i < nu_ref[0])
    def _():
        tmg = xb.shape[0]

        @pl.when(j == 0)
        def _():
            for c in range(ROW_TILE):
                xb[:, c * LANES:(c + 1) * LANES] = x_ref[pl.ds(c, tmg, stride=ROW_TILE), :].astype(BF16)
            acc[...] = jnp.zeros_like(acc)

        x = xb[...]
        g = jnp.dot(x, wg_ref[...], preferred_element_type=F32)
        u = jnp.dot(x, wu_ref[...], preferred_element_type=F32)
        acc[...] += jnp.dot(_swiglu_act(g, u).astype(BF16), wd_ref[...], preferred_element_type=F32)

        @pl.when(j == nj - 1)
        def _():
            for c in range(ROW_TILE):
                y_ref[pl.ds(c, tmg, stride=ROW_TILE), :] = acc[:, c * LANES:(c + 1) * LANES]

    @pl.when(jnp.logical_and(i >= nu_ref[0], j == nj - 1))
    def _():
        y_ref[...] = jnp.zeros_like(y_ref)


def _grouped_ffn(te, nused, xs, wg, wu, wd, tmg, fc, nt):
    d, f = wg.shape[1], wg.shape[2]
    nj = f // fc
    blk = tmg * ROW_TILE

    def tile(i, nu):
        return jnp.minimum(i, nu[0] - 1)

    def chunk(i, j, nu):
        return jnp.where(i < nu[0], j, nj - 1)

    grid_spec = pltpu.PrefetchScalarGridSpec(
        num_scalar_prefetch=2,
        grid=(nt, nj),
        in_specs=[
            pl.BlockSpec((blk, LANES), lambda i, j, te, nu: (tile(i, nu), 0)),
            pl.BlockSpec((None, d, fc), lambda i, j, te, nu: (te[tile(i, nu)], 0, chunk(i, j, nu))),
            pl.BlockSpec((None, d, fc), lambda i, j, te, nu: (te[tile(i, nu)], 0, chunk(i, j, nu))),
            pl.BlockSpec((None, fc, d), lambda i, j, te, nu: (te[tile(i, nu)], chunk(i, j, nu), 0)),
        ],
        out_specs=pl.BlockSpec((blk, LANES), lambda i, j, te, nu: (i, 0)),
        scratch_shapes=[pltpu.VMEM((tmg, d), BF16), pltpu.VMEM((tmg, d), F32)],
    )
    return pl.pallas_call(
        _grouped_ffn_kernel,
        grid_spec=grid_spec,
        out_shape=jax.ShapeDtypeStruct(xs.shape, F32),
        compiler_params=_cparams(("arbitrary", "arbitrary")),
        name="moe_ffn",
    )(te, nused, xs, wg, wu, wd)


def _combine_kernel(ntab, basetab, gstart, ys_hbm, slot_ref, h_ref, gain_ref, o_ref, ybuf, yg, sem, *, tm):
    i = pl.program_id(0)
    half = lax.rem(i, 2)

    def runs(tile, which, act):
        def go(loc, src, size):
            cp = pltpu.make_async_copy(_rows(ys_hbm, src, size), _rows(ybuf, which * (2 * tm) + loc, size),
                                       sem.at[which])
            getattr(cp, act)()
        _tile_runs(tile, ntab, basetab, gstart, tm, go)

    @pl.when(i == 0)
    def _():
        runs(i, half, "start")

    @pl.when(i + 1 < pl.num_programs(0))
    def _():
        runs(i + 1, 1 - half, "start")

    runs(i, half, "wait")
    sub0 = pl.multiple_of(half * (2 * tm * ROW_TILE), ROW_TILE)
    for c in range(ROW_TILE):
        yg[:, c * LANES:(c + 1) * LANES] = ybuf[pl.ds(sub0 + c, 2 * tm, stride=ROW_TILE), :].astype(BF16)

    sw = jnp.concatenate([slot_ref[...], jnp.zeros((LANES - 8, tm), F32)], axis=0).T
    lane = lax.broadcasted_iota(jnp.int32, (tm, 2 * tm), 1).astype(F32)
    y = h_ref[...]
    for k in range(2):
        pick = jnp.where(lane == sw[:, k:k + 1], 1.0, 0.0).astype(BF16)
        y = y + sw[:, 2 + k:3 + k] * jnp.dot(pick, yg[...], preferred_element_type=F32)
    o_ref[...] = _rms(y, gain_ref[...])


def _combine(ntab, basetab, gstart, ys, slotw, h, gain, tm):
    t, d = h.shape
    grid_spec = pltpu.PrefetchScalarGridSpec(
        num_scalar_prefetch=3,
        grid=(t // tm,),
        in_specs=[pl.BlockSpec(memory_space=pl.ANY),
                  pl.BlockSpec((8, tm), lambda i, *_: (0, i)),
                  pl.BlockSpec((tm, d), lambda i, *_: (i, 0)),
                  pl.BlockSpec((1, d), lambda i, *_: (0, 0))],
        out_specs=pl.BlockSpec((tm, d), lambda i, *_: (i, 0)),
        scratch_shapes=[pltpu.VMEM((2 * 2 * tm * ROW_TILE, LANES), F32), pltpu.VMEM((2 * tm, d), BF16),
                        pltpu.SemaphoreType.DMA((2,))],
    )
    return pl.pallas_call(
        functools.partial(_combine_kernel, tm=tm),
        grid_spec=grid_spec,
        out_shape=jax.ShapeDtypeStruct((t, d), F32),
        compiler_params=_cparams(("arbitrary",)),
        name="moe_combine",
    )(ntab, basetab, gstart, ys, slotw, h, gain)


def _half_if_aligned(n):
    return n // 2 if n % (2 * LANES) == 0 else n


def _tiles(seq, d_ff, d_ff_expert):
    return dict(
        tm_in=512,
        tq=min(1024, seq),
        tm_mg=512,
        tm_ffn=512, fc_ffn=_half_if_aligned(d_ff),
        tm_sc=512,
        tmg=512, fc_moe=_half_if_aligned(d_ff_expert),
    )


def kernel(x, meta_tokens, norm_mix, w_in, attn_sinks, w_attn_br, w_pool_grp, pool_scale, w_out, norm_ffn,
           dense_w_gate, dense_w_up, dense_w_down, moe_router, moe_w_gate, moe_w_up, moe_w_down, norm_final):
    batch, seq, d = x.shape
    depth = w_in.shape[0]
    assert depth == 2 and dense_w_gate.shape[0] == 1 and moe_router.shape[0] == 1, "dense layer then expert layer"
    assert moe_router.shape[2] == N_EXPERTS and w_in.shape[2] == N_HEADS * HEAD_DIM + 2 * KV_WIDTH + POOL_WIDTH + 2 * d
    t = batch * seq
    cfg = _tiles(seq, dense_w_gate.shape[2], moe_w_gate.shape[3])
    bf = lambda a: a.astype(BF16)

    bias_np, mrow_np, sinkpos_np = _attn_bias_tables()
    mrow = jnp.asarray(mrow_np)
    pool_inv_main = jnp.asarray(_pool_inverse_counts(cfg["tm_mg"], False))
    pool_inv_meta = jnp.asarray(_pool_inverse_counts(META_ROWS, True))

    h = x.reshape(t, d)
    hm = jnp.concatenate([meta_tokens.astype(F32), jnp.zeros((META_ROWS - N_META, d), F32)], axis=0)

    out = None
    for layer in range(depth):
        gain = norm_mix[layer].reshape(1, d)
        w_in_bf = bf(w_in[layer])
        sink_row = jnp.repeat(attn_sinks[layer].astype(F32) * LOG2E, KCOLS).reshape(N_HEADS // 2, 1, 2 * KCOLS)
        bias = jnp.asarray(bias_np) + (jnp.asarray(sinkpos_np) * sink_row)[None]
        wa, wo = bf(w_attn_br[layer]), bf(w_out[layer])
        wp = bf(w_pool_grp[layer])
        ps = pool_scale[layer].reshape(1, d)
        gain_ffn = norm_ffn[layer].reshape(1, d)

        qm, kkm_all, vvm_all, um, gm = _inproj(hm, gain, w_in_bf, META_ROWS)
        kkm, vvm = kkm_all[:N_META], vvm_all[:N_META]
        um16 = um[:N_META]
        q, kk, vv, u, g = _inproj(h, gain, w_in_bf, cfg["tm_in"])
        attn = _attention(q, kk, vv, kkm, vvm, bias, mrow, batch=batch, seq=seq, tq=cfg["tq"], meta_mode=False)

        u_before = um16
        if layer == 0:
            attn_m = _attention(qm, kkm_all, vvm_all, kkm, vvm, bias, mrow,
                                batch=1, seq=META_ROWS, tq=META_ROWS, meta_mode=True)
            hm, hnm = _merge(attn_m, um, jnp.zeros_like(u_before), gm, hm, wa, wp, ps, wo, pool_inv_meta, gain_ffn,
                             None, batch=1, seq=META_ROWS, tm=META_ROWS)
            h, hn = _merge(attn, u, u_before, g, h, wa, wp, ps, wo, pool_inv_main, gain_ffn, None,
                           batch=batch, seq=seq, tm=cfg["tm_mg"])
            wg, wu, wd = bf(dense_w_gate[0]), bf(dense_w_up[0]), bf(dense_w_down[0])
            hm = _dense_ffn(hnm, hm, wg, wu, wd, META_ROWS, cfg["fc_ffn"])
            h = _dense_ffn(hn, h, wg, wu, wd, cfg["tm_ffn"], cfg["fc_ffn"])
        else:
            r = moe_router[0].astype(F32)
            r_hi = r.astype(BF16)
            r_lo = (r - r_hi.astype(F32)).astype(BF16)
            rt = jnp.concatenate([r_hi.T, r_lo.T], axis=0)
            h, hn, ri, rw = _merge(attn, u, u_before, g, h, wa, wp, ps, wo, pool_inv_main, gain_ffn, rt,
                                   batch=batch, seq=seq, tm=cfg["tm_mg"])
            wg, wu, wd = bf(moe_w_gate[0]), bf(moe_w_up[0]), bf(moe_w_down[0])
            out = _moe(h, hn, ri, rw, wg, wu, wd, norm_final.reshape(1, d), cfg)
    return out.reshape(batch, seq, d)


def _moe(h, hn, ri, rw, wg, wu, wd, gain_final, cfg):
    t, d = h.shape
    tmg, tm = cfg["tmg"], cfg["tm_sc"]
    nt = 2 * t // tmg + N_EXPERTS
    nt_pad = -(-nt // LANES) * LANES
    slotw, tab, fin = _route_tables(ri, rw, tm, tmg, nt_pad)
    ntab = tab[:, 0, :E_ROWS].reshape(-1)
    basetab = tab[:, 1, :E_ROWS].reshape(-1)
    te, nused, gstart, gcount = fin[0, :nt], fin[1, :1], fin[2, :E_ROWS], fin[3, :E_ROWS]
    xs = _scatter_rows(ntab, basetab, gstart, gcount, hn, slotw, nt * tmg, tm, tmg)
    ys = _grouped_ffn(te, nused, xs, wg, wu, wd, tmg, cfg["fc_moe"], nt)
    return _combine(ntab, basetab, gstart, ys, slotw, h, gain_final, tm)
```

```python
import functools
import math

import numpy as np
import jax
import jax.numpy as jnp
from jax import lax
from jax.experimental import pallas as pl
from jax.experimental.pallas import tpu as pltpu

F32 = jnp.float32
BF16 = jnp.bfloat16

N_HEADS = 16
HEAD_DIM = 64
N_KV_HEADS = 2
KV_WIDTH = N_KV_HEADS * HEAD_DIM
WINDOW = 128
N_META = 16
POOL_WINDOWS = (2, 4, 8, 16)
N_POOL_GROUPS = 4
POOL_GROUP_DIM = 128
POOL_WIDTH = N_POOL_GROUPS * POOL_GROUP_DIM
N_EXPERTS = 8
RMS_EPS = 1e-5
NEG_BIAS = -1e30
LOG2E = 1.4426950408889634

LANES = 128
META_ROWS = 128
VMEM_LIMIT = 56 * 1024 * 1024

QB = 64
KB = WINDOW + QB
KCOLS = 256
SINK_COL = KB + N_META


def _cparams(sem, vmem=VMEM_LIMIT):
    return pltpu.CompilerParams(dimension_semantics=sem, vmem_limit_bytes=vmem)


def _resident(shape):
    zeros = (0,) * len(shape)
    return pl.BlockSpec(shape, lambda *_: zeros, pipeline_mode=pl.Buffered(1))


def _rms(x, gain):
    ms = jnp.mean(x * x, axis=-1, keepdims=True)
    return x * lax.rsqrt(ms + RMS_EPS) * gain


def _inproj_kernel(h_ref, gain_ref, w_ref, q_ref, kk_ref, vv_ref, u_ref, g_ref, *, d_model):
    xn = _rms(h_ref[...], gain_ref[...]).astype(BF16)
    aw = N_HEADS * HEAD_DIM
    qscale = (HEAD_DIM ** -0.5) * LOG2E
    half = aw // 2
    for c in range(2):
        q = jnp.dot(xn, w_ref[:, c * half:(c + 1) * half], preferred_element_type=F32)
        q_ref[:, c * half:(c + 1) * half] = (q * qscale).astype(BF16)
    kv = jnp.dot(xn, w_ref[:, aw:aw + 2 * KV_WIDTH], preferred_element_type=F32)
    k = kv[:, :KV_WIDTH]
    v = kv[:, KV_WIDTH:]
    kk_ref[:, :KV_WIDTH] = k.astype(BF16)
    kk_ref[:, KV_WIDTH:] = pltpu.roll(k, HEAD_DIM, 1).astype(BF16)
    vv_ref[:, :KV_WIDTH] = v.astype(BF16)
    vv_ref[:, KV_WIDTH:] = pltpu.roll(v, HEAD_DIM, 1).astype(BF16)
    o = aw + 2 * KV_WIDTH
    u_ref[...] = jnp.dot(xn, w_ref[:, o:o + POOL_WIDTH], preferred_element_type=F32).astype(BF16)
    o += POOL_WIDTH
    gw = 2 * d_model
    gc = 512
    for c in range(gw // gc):
        g = jnp.dot(xn, w_ref[:, o + c * gc:o + (c + 1) * gc], preferred_element_type=F32)
        g_ref[:, c * gc:(c + 1) * gc] = jax.nn.sigmoid(g).astype(BF16)


def _inproj(h, gain, w_bf, tm):
    t, d = h.shape
    n = w_bf.shape[1]
    aw = N_HEADS * HEAD_DIM
    row = lambda w: pl.BlockSpec((tm, w), lambda i: (i, 0))
    return pl.pallas_call(
        functools.partial(_inproj_kernel, d_model=d),
        grid=(t // tm,),
        in_specs=[row(d), _resident((1, d)), _resident((d, n))],
        out_specs=[row(aw), row(2 * KV_WIDTH), row(2 * KV_WIDTH), row(POOL_WIDTH), row(2 * d)],
        out_shape=[
            jax.ShapeDtypeStruct((t, aw), BF16),
            jax.ShapeDtypeStruct((t, 2 * KV_WIDTH), BF16),
            jax.ShapeDtypeStruct((t, 2 * KV_WIDTH), BF16),
            jax.ShapeDtypeStruct((t, POOL_WIDTH), BF16),
            jax.ShapeDtypeStruct((t, 2 * d), BF16),
        ],
        compiler_params=_cparams(("parallel",)),
        name="inproj",
    )(h, gain, w_bf)


def _attn_bias_tables():
    slopes = np.array([2.0 ** (-8.0 * (h + 1) / N_HEADS) for h in range(N_HEADS)], dtype=np.float64)
    i = np.arange(QB)[:, None]
    c = np.arange(KCOLS)[None, :]
    d_band = WINDOW + i - c
    ok_band = (c < KB) & (d_band >= 0) & (d_band < WINDOW)
    m = c - KB
    is_meta = (c >= KB) & (c < KB + N_META)
    d_meta = N_META + i - m
    tbl = np.full((4, N_HEADS, QB, KCOLS), NEG_BIAS, dtype=np.float64)
    for var in range(4):
        if var == 0:
            okb = ok_band
        elif var == 1:
            okb = ok_band & (c >= WINDOW)
        elif var == 2:
            okb = ok_band & (c >= WINDOW - QB)
        else:
            okb = np.zeros_like(ok_band)
        okm = np.broadcast_to(is_meta, (QB, KCOLS)) & ((d_meta - N_META >= 0) if var == 3 else True)
        okb = np.broadcast_to(okb, (QB, KCOLS))
        for h in range(N_HEADS):
            t = tbl[var, h]
            t[okb] = (-slopes[h] * LOG2E * d_band)[okb]
            t[okm] = (-slopes[h] * LOG2E * d_meta)[okm]
            t[:, SINK_COL] = 0.0
    tbl = tbl.reshape(4, N_HEADS // 2, 2, QB, KCOLS).transpose(0, 1, 3, 2, 4)
    tbl = tbl.reshape(4, N_HEADS // 2, QB, 2 * KCOLS)
    mrow = np.zeros((N_HEADS, KCOLS), dtype=np.float64)
    mrow[:, KB:KB + N_META] = (-slopes * LOG2E)[:, None]
    mrow = mrow.reshape(N_HEADS // 2, 1, 2 * KCOLS)
    sink_pos = np.zeros((N_HEADS // 2, 1, 2 * KCOLS), dtype=np.float32)
    sink_pos[:, 0, SINK_COL] = 1.0
    sink_pos[:, 0, KCOLS + SINK_COL] = 1.0
    return tbl.astype(np.float32), mrow.astype(np.float32), sink_pos


def _attn_kernel(q_ref, kk_ref, vv_ref, kkh_ref, vvh_ref, kkm_ref, vvm_ref, bias_ref, mrow_ref,
                 o_ref, kcat, vcat, bmat, vmat, *, tq, meta_mode):
    hb_per_iter = bmat.shape[0]
    j = pl.program_id(1)
    hd = HEAD_DIM

    kcat[0:WINDOW, :] = kkh_ref[...]
    kcat[WINDOW:, :] = kk_ref[...]
    vcat[0:WINDOW, :] = vvh_ref[...]
    vcat[WINDOW:, :] = vv_ref[...]

    def place(dst, par, src, rows, kvh, row0, nrows):
        lo_col = 0 if kvh == 0 else 2 * hd
        hi_col = 3 * hd if kvh == 0 else hd
        dst[par, kvh, row0:row0 + nrows, 0:hd] = src[rows, lo_col:lo_col + hd]
        dst[par, kvh, KCOLS + row0:KCOLS + row0 + nrows, hd:2 * hd] = src[rows, hi_col:hi_col + hd]

    @pl.when(jnp.logical_and(pl.program_id(0) == 0, j == 0))
    def _():
        bmat[...] = jnp.zeros_like(bmat)
        for par in range(hb_per_iter):
            for kvh in range(N_KV_HEADS):
                vmat[par, kvh, :, 0:LANES] = jnp.zeros((2 * KCOLS, LANES), BF16)
                vmat[par, kvh, 0:KCOLS, LANES:LANES + hd] = jnp.ones((KCOLS, hd), BF16)
                vmat[par, kvh, 0:KCOLS, LANES + hd:2 * LANES] = jnp.zeros((KCOLS, hd), BF16)
                vmat[par, kvh, KCOLS:2 * KCOLS, LANES:LANES + hd] = jnp.zeros((KCOLS, hd), BF16)
                vmat[par, kvh, KCOLS:2 * KCOLS, LANES + hd:2 * LANES] = jnp.ones((KCOLS, hd), BF16)
                place(bmat, par, kkm_ref, slice(None), kvh, KB, N_META)
                place(vmat, par, vvm_ref, slice(None), kvh, KB, N_META)

    ppk = N_HEADS // 2 // N_KV_HEADS

    def half_block(s, par):
        o = pl.multiple_of(s * QB, QB)
        if meta_mode:
            var = 3
            pos0 = jnp.asarray(s * QB - N_META, dtype=F32)
        else:
            first = jnp.logical_and(j == 0, s < 2)
            var = jnp.where(first, s + 1, 0)
            pos0 = (j * tq + s * QB).astype(F32)
        for kvh in range(N_KV_HEADS):
            place(bmat, par, kcat, pl.ds(o, KB), kvh, 0, KB)
            place(vmat, par, vcat, pl.ds(o, KB), kvh, 0, KB)
        for kvh in range(N_KV_HEADS):
            pairs = range(kvh * ppk, (kvh + 1) * ppk)
            qs = jnp.concatenate([q_ref[pl.ds(o, QB), p * LANES:(p + 1) * LANES] for p in pairs], axis=0)
            sc = lax.dot_general(qs, bmat[par, kvh], (((1,), (1,)), ((), ())), preferred_element_type=F32)
            probs = []
            for n, p in enumerate(pairs):
                sp = sc[n * QB:(n + 1) * QB, :] + (bias_ref[var, p] + mrow_ref[p] * pos0)
                m0 = jnp.max(sp[:, :KCOLS], axis=1, keepdims=True)
                m1 = jnp.max(sp[:, KCOLS:], axis=1, keepdims=True)
                pr = jnp.concatenate([jnp.exp2(sp[:, :KCOLS] - m0), jnp.exp2(sp[:, KCOLS:] - m1)], axis=1)
                probs.append(pr.astype(BF16))
            ov = jnp.dot(jnp.concatenate(probs, axis=0), vmat[par, kvh], preferred_element_type=F32)
            for n, p in enumerate(pairs):
                on = ov[n * QB:(n + 1) * QB, :]
                o_ref[pl.ds(o, QB), p * LANES:(p + 1) * LANES] = (on[:, :LANES] / on[:, LANES:]).astype(BF16)

    def group(sg, carry):
        for n in range(hb_per_iter):
            half_block(hb_per_iter * sg + n, n)
        return carry

    lax.fori_loop(0, tq // (hb_per_iter * QB), group, 0)


def _attention(q, kk, vv, kkm, vvm, bias, mrow, *, batch, seq, tq, meta_mode):
    aw = N_HEADS * HEAD_DIM
    nj = seq // tq
    hb = tq // WINDOW if not meta_mode else 1
    hb_per_iter = min(16, tq // QB)

    def row_map(b, j):
        return (b * nj + j, 0)

    def halo_map(b, j):
        if meta_mode:
            return (0, 0)
        return (jnp.maximum(b * nj * hb + j * hb - 1, b * nj * hb), 0)

    const2 = lambda b, j: (0, 0)
    return pl.pallas_call(
        functools.partial(_attn_kernel, tq=tq, meta_mode=meta_mode),
        grid=(batch, nj),
        in_specs=[
            pl.BlockSpec((tq, aw), row_map),
            pl.BlockSpec((tq, 2 * KV_WIDTH), row_map),
            pl.BlockSpec((tq, 2 * KV_WIDTH), row_map),
            pl.BlockSpec((WINDOW, 2 * KV_WIDTH), halo_map),
            pl.BlockSpec((WINDOW, 2 * KV_WIDTH), halo_map),
            pl.BlockSpec((N_META, 2 * KV_WIDTH), const2),
            pl.BlockSpec((N_META, 2 * KV_WIDTH), const2),
            pl.BlockSpec(bias.shape, lambda b, j: (0, 0, 0, 0)),
            pl.BlockSpec(mrow.shape, lambda b, j: (0, 0, 0)),
        ],
        out_specs=pl.BlockSpec((tq, aw), row_map),
        out_shape=jax.ShapeDtypeStruct((batch * seq, aw), BF16),
        scratch_shapes=[
            pltpu.VMEM((WINDOW + tq, 2 * KV_WIDTH), BF16),
            pltpu.VMEM((WINDOW + tq, 2 * KV_WIDTH), BF16),
            pltpu.VMEM((hb_per_iter, N_KV_HEADS, 2 * KCOLS, LANES), BF16),
            pltpu.VMEM((hb_per_iter, N_KV_HEADS, 2 * KCOLS, 2 * LANES), BF16),
        ],
        compiler_params=_cparams(("arbitrary", "arbitrary")),
        name="attn_meta" if meta_mode else "attn",
    )(q, kk, vv, kk, vv, kkm, vvm, bias, mrow)


POOL_HALO = 16


def _pool_inverse_counts(tm, clip_at_row0):
    t = np.arange(tm, dtype=np.float64)[None, :, None]
    w = np.asarray(POOL_WINDOWS, dtype=np.float64)[:, None, None]
    cnt = np.minimum(t + 1, w) if clip_at_row0 else np.broadcast_to(w, (N_POOL_GROUPS, tm, 1))
    return (1.0 / cnt).astype(np.float32)


def _merge_kernel(*refs, with_router, fc):
    (attn_ref, u_ref, uh_ref, um_ref, g_ref, h_ref, wa_ref, wp_ref, ps_ref, wo_ref, pinv_ref,
     gain_ref) = refs[:12]
    if with_router:
        rt_ref, hout_ref, hn_ref, ri_ref, rw_ref = refs[12:]
    else:
        wg_ref, wu_ref, wd_ref, hout_ref = refs[12:]

    j = pl.program_id(1)
    d = h_ref.shape[1]
    a = jnp.dot(attn_ref[...], wa_ref[...], preferred_element_type=F32)

    before = jnp.where(j == 0, um_ref[...], uh_ref[...])
    x = jnp.concatenate([before, u_ref[...]], axis=0).astype(F32)
    parts = []
    for g, w in enumerate(POOL_WINDOWS):
        xg = x[:, g * POOL_GROUP_DIM:(g + 1) * POOL_GROUP_DIM]
        s, k = xg, 1
        while k < w:
            s = s + pltpu.roll(s, k, 0)
            k *= 2
        pooled = s[POOL_HALO:] * pinv_ref[g] - xg[POOL_HALO:]
        parts.append(jnp.dot(pooled.astype(BF16), wp_ref[g], preferred_element_type=F32))
    pb = jnp.concatenate(parts, axis=1) * ps_ref[...]
    gates = g_ref[...]
    merged = gates[:, :d].astype(F32) * a + gates[:, d:].astype(F32) * pb
    out = jnp.dot(merged.astype(BF16), wo_ref[...], preferred_element_type=F32)
    hnew = h_ref[...] + out
    xn = _rms(hnew, gain_ref[...])
    xh = xn.astype(BF16)
    if not with_router:
        hout_ref[...] = _dense_swiglu(xh, hnew, wg_ref, wu_ref, wd_ref, fc)
    else:
        hout_ref[...] = hnew
        hn_ref[...] = xh
        xl = (xn - xh.astype(F32)).astype(BF16)
        nt = (((1,), (1,)), ((), ()))
        t_hi = lax.dot_general(rt_ref[...], xh, nt, preferred_element_type=F32)
        t_lo = lax.dot_general(rt_ref[...], xl, nt, preferred_element_type=F32)
        lg = t_hi[:N_EXPERTS] + t_hi[N_EXPERTS:] + t_lo[:N_EXPERTS]
        eid = lax.broadcasted_iota(jnp.int32, lg.shape, 0).astype(F32)
        m1 = jnp.max(lg, axis=0, keepdims=True)
        i1 = jnp.min(jnp.where(lg == m1, eid, float(N_EXPERTS)), axis=0, keepdims=True)
        lg2 = jnp.where(eid == i1, -jnp.inf, lg)
        m2 = jnp.max(lg2, axis=0, keepdims=True)
        i2 = jnp.min(jnp.where(lg2 == m2, eid, float(N_EXPERTS)), axis=0, keepdims=True)
        e2 = jnp.exp(m2 - m1)
        w1 = 1.0 / (1.0 + e2)
        w2 = e2 / (1.0 + e2)
        row = lax.broadcasted_iota(jnp.int32, lg.shape, 0)
        ri_ref[...] = jnp.where(row == 0, i1, jnp.where(row == 1, i2, 0.0)).astype(jnp.int32)
        rw_ref[...] = jnp.where(row == 0, w1, jnp.where(row == 1, w2, 0.0))


def _merge(attn, u, u_meta, gates, h, wa, wp, pscale, wo, pool_inv, gain, *, rt=None, ffn=None, fc=None,
           batch, seq, tm):
    t, d = h.shape
    nj = seq // tm
    hs = tm // POOL_HALO
    with_router = rt is not None
    assert with_router != (ffn is not None)

    def row_map(b, j):
        return (b * nj + j, 0)

    def halo_map(b, j):
        return (jnp.maximum((b * nj + j) * hs - 1, 0), 0)

    in_specs = [
        pl.BlockSpec((tm, attn.shape[1]), row_map),
        pl.BlockSpec((tm, POOL_WIDTH), row_map),
        pl.BlockSpec((POOL_HALO, POOL_WIDTH), halo_map),
        _resident((POOL_HALO, POOL_WIDTH)),
        pl.BlockSpec((tm, 2 * d), row_map),
        pl.BlockSpec((tm, d), row_map),
        _resident(wa.shape),
        _resident(wp.shape),
        _resident((1, d)),
        _resident(wo.shape),
        _resident(pool_inv.shape),
        _resident((1, d)),
    ]
    args = [attn, u, u, u_meta, gates, h, wa, wp, pscale, wo, pool_inv, gain]
    out_specs = [pl.BlockSpec((tm, d), row_map)]
    out_shape = [jax.ShapeDtypeStruct((t, d), F32)]
    if with_router:
        in_specs.append(_resident(rt.shape))
        args.append(rt)
        lane_map = lambda b, j: (0, b * nj + j)
        out_specs += [pl.BlockSpec((tm, d), row_map), pl.BlockSpec((8, tm), lane_map), pl.BlockSpec((8, tm), lane_map)]
        out_shape += [jax.ShapeDtypeStruct((t, d), BF16), jax.ShapeDtypeStruct((8, t), jnp.int32),
                      jax.ShapeDtypeStruct((8, t), F32)]
    else:
        in_specs += [_resident(w.shape) for w in ffn]
        args += list(ffn)
    return pl.pallas_call(
        functools.partial(_merge_kernel, with_router=with_router, fc=fc),
        grid=(batch, nj),
        in_specs=in_specs,
        out_specs=out_specs,
        out_shape=out_shape,
        compiler_params=_cparams(("parallel", "parallel")),
        name="merge_router" if with_router else "merge_ffn",
    )(*args)


def _swiglu_act(g, u):
    return (g * jax.nn.sigmoid(g)) * u


def _dense_swiglu(x, residual, wg_ref, wu_ref, wd_ref, fc):
    acc = residual
    for c in range(wg_ref.shape[1] // fc):
        cols = slice(c * fc, (c + 1) * fc)
        g = jnp.dot(x, wg_ref[:, cols], preferred_element_type=F32)
        u = jnp.dot(x, wu_ref[:, cols], preferred_element_type=F32)
        acc = acc + jnp.dot(_swiglu_act(g, u).astype(BF16), wd_ref[cols, :], preferred_element_type=F32)
    return acc


E_ROWS = 16


def _col_to_lanes(col, width):
    r = lax.broadcasted_iota(jnp.int32, (E_ROWS, width), 0)
    c = lax.broadcasted_iota(jnp.int32, (E_ROWS, width), 1)
    return jnp.sum(jnp.where(r == c, col, 0.0), axis=0, keepdims=True)


def _excl_cumsum_col(col):
    r = lax.broadcasted_iota(jnp.int32, (E_ROWS, E_ROWS), 0)
    c = lax.broadcasted_iota(jnp.int32, (E_ROWS, E_ROWS), 1)
    return jnp.sum(jnp.where(c < r, _col_to_lanes(col, E_ROWS), 0.0), axis=1, keepdims=True)


def _route_tab_kernel(ri_ref, rw_ref, tri_ref, slot_ref, tab_ref, fin_ref, carry, *, tn, tmg, nt_pad):
    i = pl.program_id(0)

    @pl.when(i == 0)
    def _():
        carry[...] = jnp.zeros_like(carry)

    eid = lax.broadcasted_iota(jnp.int32, (E_ROWS, tn), 0)
    ri = ri_ref[...]
    oh0 = (eid == ri[0:1, :]).astype(F32)
    oh1 = (eid == ri[1:2, :]).astype(F32)
    oh = oh0 + oh1
    cum = jnp.dot(oh.astype(BF16), tri_ref[...], preferred_element_type=F32)
    cnt = jnp.sum(oh, axis=1, keepdims=True)
    base = _excl_cumsum_col(cnt) + cum
    s0 = jnp.sum(oh0 * base, axis=0, keepdims=True)
    s1 = jnp.sum(oh1 * base, axis=0, keepdims=True)
    row = lax.broadcasted_iota(jnp.int32, (8, tn), 0)
    rw = rw_ref[...]
    slot_ref[...] = jnp.where(row == 0, s0, jnp.where(row == 1, s1, jnp.where(row == 2, rw[0:1, :],
                              jnp.where(row == 3, rw[1:2, :], 0.0))))
    trow = lax.broadcasted_iota(jnp.int32, (8, LANES), 0)
    tab_ref[...] = jnp.where(trow == 0, _col_to_lanes(cnt, LANES),
                             jnp.where(trow == 1, _col_to_lanes(carry[...], LANES), 0.0)).astype(jnp.int32)
    carry[...] += cnt

    @pl.when(i == pl.num_programs(0) - 1)
    def _():
        total = carry[...]
        padded = jnp.ceil(total / tmg) * tmg
        st = _excl_cumsum_col(padded)
        ends = st + padded
        tile0 = (lax.broadcasted_iota(jnp.int32, (E_ROWS, nt_pad), 1) * tmg).astype(F32)
        texp = jnp.minimum(jnp.sum((tile0 >= ends).astype(F32), axis=0, keepdims=True), N_EXPERTS - 1.0)
        nused = jnp.max(ends, axis=0, keepdims=True) / tmg
        frow = lax.broadcasted_iota(jnp.int32, (8, nt_pad), 0)
        fin_ref[...] = jnp.where(frow == 0, texp, jnp.where(frow == 1, nused, jnp.where(
            frow == 2, _col_to_lanes(st, nt_pad), jnp.where(frow == 3, _col_to_lanes(total, nt_pad), 0.0)))
        ).astype(jnp.int32)


def _route_tables(ri, rw, tn, tmg, nt_pad):
    t = ri.shape[1]
    nb = t // tn
    tri = jnp.asarray(np.triu(np.ones((tn, tn), dtype=np.float32), k=1), dtype=BF16)
    lane_blk = pl.BlockSpec((8, tn), lambda i: (0, i))
    return pl.pallas_call(
        functools.partial(_route_tab_kernel, tn=tn, tmg=tmg, nt_pad=nt_pad),
        grid=(nb,),
        in_specs=[lane_blk, lane_blk, pl.BlockSpec((tn, tn), lambda i: (0, 0))],
        out_specs=[lane_blk, pl.BlockSpec((None, 8, LANES), lambda i: (i, 0, 0)),
                   pl.BlockSpec((8, nt_pad), lambda i: (0, 0))],
        out_shape=[jax.ShapeDtypeStruct((8, t), F32), jax.ShapeDtypeStruct((nb, 8, LANES), jnp.int32),
                   jax.ShapeDtypeStruct((8, nt_pad), jnp.int32)],
        scratch_shapes=[pltpu.VMEM((E_ROWS, 1), F32)],
        compiler_params=_cparams(("arbitrary",)),
        name="route_tab",
    )(ri, rw, tri)


ROW_TILE = 8


def _pieces(n, max_log2, fn):
    for b in reversed(range(max_log2 + 1)):
        above = (n >> (b + 1)) << (b + 1)

        @pl.when(((n >> b) & 1) == 1)
        def _():
            fn(above, 1 << b)


def _tile_runs(i, ntab, basetab, gstart, tm, fn):
    off = jnp.int32(0)
    for e in range(N_EXPERTS):
        n_e = ntab[i * E_ROWS + e]
        dst = gstart[e] + basetab[i * E_ROWS + e]
        _pieces(n_e, tm.bit_length() - 1, lambda o, size, off=off, dst=dst: fn(off + o, dst + o, size))
        off = off + n_e


def _rows(ref, start, size):
    return ref.at[pl.ds(pl.multiple_of(start * ROW_TILE, ROW_TILE), size * ROW_TILE)]


def _scatter_kernel(ntab, basetab, gstart, gcount, x_ref, slot_ref, xs_hbm, sbuf, zbuf, sem, zsem, *, tm, tmg):
    i = pl.program_id(0)
    last = pl.num_programs(0) - 1
    n_sorted = xs_hbm.shape[0] // ROW_TILE
    half = lax.rem(i, 2)
    slots = slot_ref[...]
    rowid = lax.broadcasted_iota(jnp.int32, (2 * tm, tm), 0).astype(F32)
    perm = jnp.where(jnp.logical_or(rowid == slots[0:1, :], rowid == slots[1:2, :]), 1.0, 0.0).astype(BF16)
    srt = jnp.dot(perm, x_ref[...], preferred_element_type=F32)
    sub0 = pl.multiple_of(half * (2 * tm * ROW_TILE), ROW_TILE)
    for c in range(ROW_TILE):
        sbuf[pl.ds(sub0 + c, 2 * tm, stride=ROW_TILE), :] = srt[:, c * LANES:(c + 1) * LANES]

    def runs(tile, which, act):
        def go(src, dst, size):
            cp = pltpu.make_async_copy(_rows(sbuf, which * (2 * tm) + src, size), _rows(xs_hbm, dst, size),
                                       sem.at[which])
            getattr(cp, act)()
        _tile_runs(tile, ntab, basetab, gstart, tm, go)

    @pl.when(i > 0)
    def _():
        runs(i - 1, 1 - half, "wait")

    runs(i, half, "start")

    @pl.when(i == last)
    def _():
        runs(i, half, "wait")
        zbuf[...] = jnp.zeros_like(zbuf)
        zrows = zbuf.shape[0] // ROW_TILE

        def zcopy(dst, size):
            return pltpu.make_async_copy(_rows(zbuf, 0, size), _rows(xs_hbm, dst, size), zsem)

        used = gstart[N_EXPERTS]
        for act in ("start", "wait"):
            for e in range(N_EXPERTS):
                end = gstart[e] + gcount[e]
                npad = (tmg - (gcount[e] & (tmg - 1))) & (tmg - 1)
                _pieces(npad, zrows.bit_length() - 1,
                        lambda o, size, end=end, act=act: getattr(zcopy(end + o, size), act)())
            for k in range(N_EXPERTS * tmg // zrows):
                @pl.when(used + k * zrows < n_sorted)
                def _():
                    getattr(zcopy(used + k * zrows, zrows), act)()


def _scatter_rows(ntab, basetab, gstart, gcount, x, slotw, n_sorted, tm, tmg):
    t, d = x.shape
    assert d == ROW_TILE * LANES and tmg & (tmg - 1) == 0 and tm & (tm - 1) == 0
    grid_spec = pltpu.PrefetchScalarGridSpec(
        num_scalar_prefetch=4,
        grid=(t // tm,),
        in_specs=[pl.BlockSpec((tm, d), lambda i, *_: (i, 0)), pl.BlockSpec((8, tm), lambda i, *_: (0, i))],
        out_specs=pl.BlockSpec(memory_space=pl.ANY),
        scratch_shapes=[pltpu.VMEM((2 * 2 * tm * ROW_TILE, LANES), F32),
                        pltpu.VMEM((tmg // 2 * ROW_TILE, LANES), F32),
                        pltpu.SemaphoreType.DMA((2,)), pltpu.SemaphoreType.DMA],
    )
    return pl.pallas_call(
        functools.partial(_scatter_kernel, tm=tm, tmg=tmg),
        grid_spec=grid_spec,
        out_shape=jax.ShapeDtypeStruct((n_sorted * ROW_TILE, LANES), F32),
        compiler_params=_cparams(("arbitrary",)),
        name="moe_scatter",
    )(ntab, basetab, gstart, gcount, x, slotw)


def _grouped_ffn_kernel(te_ref, nu_ref, x_ref, wg_ref, wu_ref, wd_ref, y_ref, xb, acc):
    i = pl.program_id(0)
    j = pl.program_id(1)
    nj = pl.num_programs(1)

    @pl.when(i < nu_ref[0])
    def _():
        tmg = xb.shape[0]

        @pl.when(j == 0)
        def _():
            for c in range(ROW_TILE):
                xb[:, c * LANES:(c + 1) * LANES] = x_ref[pl.ds(c, tmg, stride=ROW_TILE), :].astype(BF16)
            acc[...] = jnp.zeros_like(acc)

        x = xb[...]
        g = jnp.dot(x, wg_ref[...], preferred_element_type=F32)
        u = jnp.dot(x, wu_ref[...], preferred_element_type=F32)
        acc[...] += jnp.dot(_swiglu_act(g, u).astype(BF16), wd_ref[...], preferred_element_type=F32)

        @pl.when(j == nj - 1)
        def _():
            for c in range(ROW_TILE):
                y_ref[pl.ds(c, tmg, stride=ROW_TILE), :] = acc[:, c * LANES:(c + 1) * LANES]

    @pl.when(jnp.logical_and(i >= nu_ref[0], j == nj - 1))
    def _():
        y_ref[...] = jnp.zeros_like(y_ref)


def _grouped_ffn(te, nused, xs, wg, wu, wd, tmg, fc, nt):
    d, f = wg.shape[1], wg.shape[2]
    nj = f // fc
    blk = tmg * ROW_TILE

    def tile(i, nu):
        return jnp.minimum(i, nu[0] - 1)

    def chunk(i, j, nu):
        return jnp.where(i < nu[0], j, nj - 1)

    grid_spec = pltpu.PrefetchScalarGridSpec(
        num_scalar_prefetch=2,
        grid=(nt, nj),
        in_specs=[
            pl.BlockSpec((blk, LANES), lambda i, j, te, nu: (tile(i, nu), 0)),
            pl.BlockSpec((None, d, fc), lambda i, j, te, nu: (te[tile(i, nu)], 0, chunk(i, j, nu))),
            pl.BlockSpec((None, d, fc), lambda i, j, te, nu: (te[tile(i, nu)], 0, chunk(i, j, nu))),
            pl.BlockSpec((None, fc, d), lambda i, j, te, nu: (te[tile(i, nu)], chunk(i, j, nu), 0)),
        ],
        out_specs=pl.BlockSpec((blk, LANES), lambda i, j, te, nu: (i, 0)),
        scratch_shapes=[pltpu.VMEM((tmg, d), BF16), pltpu.VMEM((tmg, d), F32)],
    )
    return pl.pallas_call(
        _grouped_ffn_kernel,
        grid_spec=grid_spec,
        out_shape=jax.ShapeDtypeStruct(xs.shape, F32),
        compiler_params=_cparams(("arbitrary", "arbitrary")),
        name="moe_ffn",
    )(te, nused, xs, wg, wu, wd)


def _combine_kernel(ntab, basetab, gstart, ys_hbm, slot_ref, h_ref, gain_ref, o_ref, ybuf, yg, sem, *, tm):
    i = pl.program_id(0)
    half = lax.rem(i, 2)

    def runs(tile, which, act):
        def go(loc, src, size):
            cp = pltpu.make_async_copy(_rows(ys_hbm, src, size), _rows(ybuf, which * (2 * tm) + loc, size),
                                       sem.at[which])
            getattr(cp, act)()
        _tile_runs(tile, ntab, basetab, gstart, tm, go)

    @pl.when(i == 0)
    def _():
        runs(i, half, "start")

    @pl.when(i + 1 < pl.num_programs(0))
    def _():
        runs(i + 1, 1 - half, "start")

    runs(i, half, "wait")
    sub0 = pl.multiple_of(half * (2 * tm * ROW_TILE), ROW_TILE)
    for c in range(ROW_TILE):
        yg[:, c * LANES:(c + 1) * LANES] = ybuf[pl.ds(sub0 + c, 2 * tm, stride=ROW_TILE), :].astype(BF16)

    sw = jnp.concatenate([slot_ref[...], jnp.zeros((LANES - 8, tm), F32)], axis=0).T
    lane = lax.broadcasted_iota(jnp.int32, (tm, 2 * tm), 1).astype(F32)
    pick = jnp.where(lane == sw[:, 0:1], sw[:, 2:3], jnp.where(lane == sw[:, 1:2], sw[:, 3:4], 0.0))
    y = h_ref[...] + jnp.dot(pick.astype(BF16), yg[...], preferred_element_type=F32)
    o_ref[...] = _rms(y, gain_ref[...])


def _combine(ntab, basetab, gstart, ys, slotw, h, gain, tm):
    t, d = h.shape
    grid_spec = pltpu.PrefetchScalarGridSpec(
        num_scalar_prefetch=3,
        grid=(t // tm,),
        in_specs=[pl.BlockSpec(memory_space=pl.ANY),
                  pl.BlockSpec((8, tm), lambda i, *_: (0, i)),
                  pl.BlockSpec((tm, d), lambda i, *_: (i, 0)),
                  pl.BlockSpec((1, d), lambda i, *_: (0, 0))],
        out_specs=pl.BlockSpec((tm, d), lambda i, *_: (i, 0)),
        scratch_shapes=[pltpu.VMEM((2 * 2 * tm * ROW_TILE, LANES), F32), pltpu.VMEM((2 * tm, d), BF16),
                        pltpu.SemaphoreType.DMA((2,))],
    )
    return pl.pallas_call(
        functools.partial(_combine_kernel, tm=tm),
        grid_spec=grid_spec,
        out_shape=jax.ShapeDtypeStruct((t, d), F32),
        compiler_params=_cparams(("arbitrary",)),
        name="moe_combine",
    )(ntab, basetab, gstart, ys, slotw, h, gain)


def _half_if_aligned(n):
    return n // 2 if n % (2 * LANES) == 0 else n


def _tiles(seq, d_ff, d_ff_expert):
    return dict(
        tm_in=1024,
        tq=min(1024, seq),
        tm_mg=512,
        fc_ffn=_half_if_aligned(d_ff),
        tm_sc=512,
        tmg=512, fc_moe=_half_if_aligned(d_ff_expert),
    )


def kernel(x, meta_tokens, norm_mix, w_in, attn_sinks, w_attn_br, w_pool_grp, pool_scale, w_out, norm_ffn,
           dense_w_gate, dense_w_up, dense_w_down, moe_router, moe_w_gate, moe_w_up, moe_w_down, norm_final):
    batch, seq, d = x.shape
    depth = w_in.shape[0]
    assert depth == 2 and dense_w_gate.shape[0] == 1 and moe_router.shape[0] == 1, "dense layer then expert layer"
    assert moe_router.shape[2] == N_EXPERTS and w_in.shape[2] == N_HEADS * HEAD_DIM + 2 * KV_WIDTH + POOL_WIDTH + 2 * d
    t = batch * seq
    cfg = _tiles(seq, dense_w_gate.shape[2], moe_w_gate.shape[3])
    bf = lambda a: a.astype(BF16)

    bias_np, mrow_np, sinkpos_np = _attn_bias_tables()
    mrow = jnp.asarray(mrow_np)
    pool_inv_main = jnp.asarray(_pool_inverse_counts(cfg["tm_mg"], False))
    pool_inv_meta = jnp.asarray(_pool_inverse_counts(META_ROWS, True))

    h = x.reshape(t, d)
    hm = jnp.concatenate([meta_tokens.astype(F32), jnp.zeros((META_ROWS - N_META, d), F32)], axis=0)

    out = None
    for layer in range(depth):
        gain = norm_mix[layer].reshape(1, d)
        w_in_bf = bf(w_in[layer])
        sink_row = jnp.repeat(attn_sinks[layer].astype(F32) * LOG2E, KCOLS).reshape(N_HEADS // 2, 1, 2 * KCOLS)
        bias = jnp.asarray(bias_np) + (jnp.asarray(sinkpos_np) * sink_row)[None]
        wa, wo = bf(w_attn_br[layer]), bf(w_out[layer])
        wp = bf(w_pool_grp[layer])
        ps = pool_scale[layer].reshape(1, d)
        gain_ffn = norm_ffn[layer].reshape(1, d)

        qm, kkm_all, vvm_all, um, gm = _inproj(hm, gain, w_in_bf, META_ROWS)
        kkm, vvm = kkm_all[:N_META], vvm_all[:N_META]
        um16 = um[:N_META]
        q, kk, vv, u, g = _inproj(h, gain, w_in_bf, cfg["tm_in"])
        attn = _attention(q, kk, vv, kkm, vvm, bias, mrow, batch=batch, seq=seq, tq=cfg["tq"], meta_mode=False)

        u_before = um16
        if layer == 0:
            attn_m = _attention(qm, kkm_all, vvm_all, kkm, vvm, bias, mrow,
                                batch=1, seq=META_ROWS, tq=META_ROWS, meta_mode=True)
            ffn = (bf(dense_w_gate[0]), bf(dense_w_up[0]), bf(dense_w_down[0]))
            hm, = _merge(attn_m, um, jnp.zeros_like(u_before), gm, hm, wa, wp, ps, wo, pool_inv_meta, gain_ffn,
                         ffn=ffn, fc=cfg["fc_ffn"], batch=1, seq=META_ROWS, tm=META_ROWS)
            h, = _merge(attn, u, u_before, g, h, wa, wp, ps, wo, pool_inv_main, gain_ffn,
                        ffn=ffn, fc=cfg["fc_ffn"], batch=batch, seq=seq, tm=cfg["tm_mg"])
        else:
            r = moe_router[0].astype(F32)
            r_hi = r.astype(BF16)
            r_lo = (r - r_hi.astype(F32)).astype(BF16)
            rt = jnp.concatenate([r_hi.T, r_lo.T], axis=0)
            h, hn, ri, rw = _merge(attn, u, u_before, g, h, wa, wp, ps, wo, pool_inv_main, gain_ffn, rt=rt,
                                   batch=batch, seq=seq, tm=cfg["tm_mg"])
            wg, wu, wd = bf(moe_w_gate[0]), bf(moe_w_up[0]), bf(moe_w_down[0])
            out = _moe(h, hn, ri, rw, wg, wu, wd, norm_final.reshape(1, d), cfg)
    return out.reshape(batch, seq, d)


def _moe(h, hn, ri, rw, wg, wu, wd, gain_final, cfg):
    t, d = h.shape
    tmg, tm = cfg["tmg"], cfg["tm_sc"]
    nt = 2 * t // tmg + N_EXPERTS
    nt_pad = -(-nt // LANES) * LANES
    slotw, tab, fin = _route_tables(ri, rw, tm, tmg, nt_pad)
    ntab = tab[:, 0, :E_ROWS].reshape(-1)
    basetab = tab[:, 1, :E_ROWS].reshape(-1)
    te, nused, gstart, gcount = fin[0, :nt], fin[1, :1], fin[2, :E_ROWS], fin[3, :E_ROWS]
    xs = _scatter_rows(ntab, basetab, gstart, gcount, hn, slotw, nt * tmg, tm, tmg)
    ys = _grouped_ffn(te, nused, xs, wg, wu, wd, tmg, cfg["fc_moe"], nt)
    return _combine(ntab, basetab, gstart, ys, slotw, h, gain_final, tm)
```

```python
import functools
import math

import numpy as np
import jax
import jax.numpy as jnp
from jax import lax
from jax.experimental import pallas as pl
from jax.experimental.pallas import tpu as pltpu

F32 = jnp.float32
BF16 = jnp.bfloat16

N_HEADS = 16
HEAD_DIM = 64
N_KV_HEADS = 2
KV_WIDTH = N_KV_HEADS * HEAD_DIM
WINDOW = 128
N_META = 16
POOL_WINDOWS = (2, 4, 8, 16)
N_POOL_GROUPS = 4
POOL_GROUP_DIM = 128
POOL_WIDTH = N_POOL_GROUPS * POOL_GROUP_DIM
N_EXPERTS = 8
RMS_EPS = 1e-5
NEG_BIAS = -1e30
LOG2E = 1.4426950408889634

LANES = 128
META_ROWS = 128
VMEM_LIMIT = 56 * 1024 * 1024

QB = 64
KB = WINDOW + QB
KCOLS = 256
SINK_COL = KB + N_META


def _cparams(sem, vmem=VMEM_LIMIT):
    return pltpu.CompilerParams(dimension_semantics=sem, vmem_limit_bytes=vmem)


def _resident(shape):
    zeros = (0,) * len(shape)
    return pl.BlockSpec(shape, lambda *_: zeros, pipeline_mode=pl.Buffered(1))


def _rms(x, gain):
    ms = jnp.mean(x * x, axis=-1, keepdims=True)
    return x * lax.rsqrt(ms + RMS_EPS) * gain


def _inproj_kernel(h_ref, gain_ref, w_ref, q_ref, kk_ref, vv_ref, u_ref, g_ref, *, d_model):
    xn = _rms(h_ref[...], gain_ref[...]).astype(BF16)
    aw = N_HEADS * HEAD_DIM
    qscale = (HEAD_DIM ** -0.5) * LOG2E
    half = aw // 2
    for c in range(2):
        q = jnp.dot(xn, w_ref[:, c * half:(c + 1) * half], preferred_element_type=F32)
        q_ref[:, c * half:(c + 1) * half] = (q * qscale).astype(BF16)
    kv = jnp.dot(xn, w_ref[:, aw:aw + 2 * KV_WIDTH], preferred_element_type=F32)
    k = kv[:, :KV_WIDTH]
    v = kv[:, KV_WIDTH:]
    kk_ref[:, :KV_WIDTH] = k.astype(BF16)
    kk_ref[:, KV_WIDTH:] = pltpu.roll(k, HEAD_DIM, 1).astype(BF16)
    vv_ref[:, :KV_WIDTH] = v.astype(BF16)
    vv_ref[:, KV_WIDTH:] = pltpu.roll(v, HEAD_DIM, 1).astype(BF16)
    o = aw + 2 * KV_WIDTH
    u_ref[...] = jnp.dot(xn, w_ref[:, o:o + POOL_WIDTH], preferred_element_type=F32).astype(BF16)
    o += POOL_WIDTH
    gw = 2 * d_model
    gc = 512
    for c in range(gw // gc):
        g = jnp.dot(xn, w_ref[:, o + c * gc:o + (c + 1) * gc], preferred_element_type=F32)
        g_ref[:, c * gc:(c + 1) * gc] = jax.nn.sigmoid(g).astype(BF16)


def _inproj(h, gain, w_bf, tm):
    t, d = h.shape
    n = w_bf.shape[1]
    aw = N_HEADS * HEAD_DIM
    row = lambda w: pl.BlockSpec((tm, w), lambda i: (i, 0))
    return pl.pallas_call(
        functools.partial(_inproj_kernel, d_model=d),
        grid=(t // tm,),
        in_specs=[row(d), _resident((1, d)), _resident((d, n))],
        out_specs=[row(aw), row(2 * KV_WIDTH), row(2 * KV_WIDTH), row(POOL_WIDTH), row(2 * d)],
        out_shape=[
            jax.ShapeDtypeStruct((t, aw), BF16),
            jax.ShapeDtypeStruct((t, 2 * KV_WIDTH), BF16),
            jax.ShapeDtypeStruct((t, 2 * KV_WIDTH), BF16),
            jax.ShapeDtypeStruct((t, POOL_WIDTH), BF16),
            jax.ShapeDtypeStruct((t, 2 * d), BF16),
        ],
        compiler_params=_cparams(("parallel",)),
        name="inproj",
    )(h, gain, w_bf)


def _attn_bias_tables():
    slopes = np.array([2.0 ** (-8.0 * (h + 1) / N_HEADS) for h in range(N_HEADS)], dtype=np.float64)
    i = np.arange(QB)[:, None]
    c = np.arange(KCOLS)[None, :]
    d_band = WINDOW + i - c
    ok_band = (c < KB) & (d_band >= 0) & (d_band < WINDOW)
    m = c - KB
    is_meta = (c >= KB) & (c < KB + N_META)
    d_meta = N_META + i - m
    tbl = np.full((4, N_HEADS, QB, KCOLS), NEG_BIAS, dtype=np.float64)
    for var in range(4):
        if var == 0:
            okb = ok_band
        elif var == 1:
            okb = ok_band & (c >= WINDOW)
        elif var == 2:
            okb = ok_band & (c >= WINDOW - QB)
        else:
            okb = np.zeros_like(ok_band)
        okm = np.broadcast_to(is_meta, (QB, KCOLS)) & ((d_meta - N_META >= 0) if var == 3 else True)
        okb = np.broadcast_to(okb, (QB, KCOLS))
        for h in range(N_HEADS):
            t = tbl[var, h]
            t[okb] = (-slopes[h] * LOG2E * d_band)[okb]
            t[okm] = (-slopes[h] * LOG2E * d_meta)[okm]
            t[:, SINK_COL] = 0.0
    tbl = tbl.reshape(4, N_HEADS // 2, 2, QB, KCOLS).transpose(0, 1, 3, 2, 4)
    tbl = tbl.reshape(4, N_HEADS // 2, QB, 2 * KCOLS)
    mrow = np.zeros((N_HEADS, KCOLS), dtype=np.float64)
    mrow[:, KB:KB + N_META] = (-slopes * LOG2E)[:, None]
    mrow = mrow.reshape(N_HEADS // 2, 1, 2 * KCOLS)
    sink_pos = np.zeros((N_HEADS // 2, 1, 2 * KCOLS), dtype=np.float32)
    sink_pos[:, 0, SINK_COL] = 1.0
    sink_pos[:, 0, KCOLS + SINK_COL] = 1.0
    return tbl.astype(np.float32), mrow.astype(np.float32), sink_pos


def _attn_kernel(q_ref, kk_ref, vv_ref, kkh_ref, vvh_ref, kkm_ref, vvm_ref, bias_ref, mrow_ref,
                 o_ref, kcat, vcat, bmat, vmat, *, tq, meta_mode):
    hb_per_iter = bmat.shape[0]
    j = pl.program_id(1)
    hd = HEAD_DIM

    kcat[0:WINDOW, :] = kkh_ref[...]
    kcat[WINDOW:, :] = kk_ref[...]
    vcat[0:WINDOW, :] = vvh_ref[...]
    vcat[WINDOW:, :] = vv_ref[...]

    def place(dst, par, src, rows, kvh, row0, nrows):
        lo_col = 0 if kvh == 0 else 2 * hd
        hi_col = 3 * hd if kvh == 0 else hd
        dst[par, kvh, row0:row0 + nrows, 0:hd] = src[rows, lo_col:lo_col + hd]
        dst[par, kvh, KCOLS + row0:KCOLS + row0 + nrows, hd:2 * hd] = src[rows, hi_col:hi_col + hd]

    @pl.when(jnp.logical_and(pl.program_id(0) == 0, j == 0))
    def _():
        bmat[...] = jnp.zeros_like(bmat)
        for par in range(hb_per_iter):
            for kvh in range(N_KV_HEADS):
                vmat[par, kvh, :, 0:LANES] = jnp.zeros((2 * KCOLS, LANES), BF16)
                vmat[par, kvh, 0:KCOLS, LANES:LANES + hd] = jnp.ones((KCOLS, hd), BF16)
                vmat[par, kvh, 0:KCOLS, LANES + hd:2 * LANES] = jnp.zeros((KCOLS, hd), BF16)
                vmat[par, kvh, KCOLS:2 * KCOLS, LANES:LANES + hd] = jnp.zeros((KCOLS, hd), BF16)
                vmat[par, kvh, KCOLS:2 * KCOLS, LANES + hd:2 * LANES] = jnp.ones((KCOLS, hd), BF16)
                place(bmat, par, kkm_ref, slice(None), kvh, KB, N_META)
                place(vmat, par, vvm_ref, slice(None), kvh, KB, N_META)

    ppk = N_HEADS // 2 // N_KV_HEADS

    def half_block(s, par):
        o = pl.multiple_of(s * QB, QB)
        if meta_mode:
            var = 3
            pos0 = jnp.asarray(s * QB - N_META, dtype=F32)
        else:
            first = jnp.logical_and(j == 0, s < 2)
            var = jnp.where(first, s + 1, 0)
            pos0 = (j * tq + s * QB).astype(F32)
        for kvh in range(N_KV_HEADS):
            place(bmat, par, kcat, pl.ds(o, KB), kvh, 0, KB)
            place(vmat, par, vcat, pl.ds(o, KB), kvh, 0, KB)
        for kvh in range(N_KV_HEADS):
            pairs = range(kvh * ppk, (kvh + 1) * ppk)
            qs = jnp.concatenate([q_ref[pl.ds(o, QB), p * LANES:(p + 1) * LANES] for p in pairs], axis=0)
            sc = lax.dot_general(qs, bmat[par, kvh], (((1,), (1,)), ((), ())), preferred_element_type=F32)
            probs = []
            for n, p in enumerate(pairs):
                sp = sc[n * QB:(n + 1) * QB, :] + (bias_ref[var, p] + mrow_ref[p] * pos0)
                m0 = jnp.max(sp[:, :KCOLS], axis=1, keepdims=True)
                m1 = jnp.max(sp[:, KCOLS:], axis=1, keepdims=True)
                pr = jnp.concatenate([jnp.exp2(sp[:, :KCOLS] - m0), jnp.exp2(sp[:, KCOLS:] - m1)], axis=1)
                probs.append(pr.astype(BF16))
            ov = jnp.dot(jnp.concatenate(probs, axis=0), vmat[par, kvh], preferred_element_type=F32)
            for n, p in enumerate(pairs):
                on = ov[n * QB:(n + 1) * QB, :]
                o_ref[pl.ds(o, QB), p * LANES:(p + 1) * LANES] = (on[:, :LANES] / on[:, LANES:]).astype(BF16)

    def group(sg, carry):
        for n in range(hb_per_iter):
            half_block(hb_per_iter * sg + n, n)
        return carry

    lax.fori_loop(0, tq // (hb_per_iter * QB), group, 0)


def _attention(q, kk, vv, kkm, vvm, bias, mrow, *, batch, seq, tq, meta_mode):
    aw = N_HEADS * HEAD_DIM
    nj = seq // tq
    hb = tq // WINDOW if not meta_mode else 1
    hb_per_iter = min(16, tq // QB)

    def row_map(b, j):
        return (b * nj + j, 0)

    def halo_map(b, j):
        if meta_mode:
            return (0, 0)
        return (jnp.maximum(b * nj * hb + j * hb - 1, b * nj * hb), 0)

    const2 = lambda b, j: (0, 0)
    return pl.pallas_call(
        functools.partial(_attn_kernel, tq=tq, meta_mode=meta_mode),
        grid=(batch, nj),
        in_specs=[
            pl.BlockSpec((tq, aw), row_map),
            pl.BlockSpec((tq, 2 * KV_WIDTH), row_map),
            pl.BlockSpec((tq, 2 * KV_WIDTH), row_map),
            pl.BlockSpec((WINDOW, 2 * KV_WIDTH), halo_map),
            pl.BlockSpec((WINDOW, 2 * KV_WIDTH), halo_map),
            pl.BlockSpec((N_META, 2 * KV_WIDTH), const2),
            pl.BlockSpec((N_META, 2 * KV_WIDTH), const2),
            pl.BlockSpec(bias.shape, lambda b, j: (0, 0, 0, 0)),
            pl.BlockSpec(mrow.shape, lambda b, j: (0, 0, 0)),
        ],
        out_specs=pl.BlockSpec((tq, aw), row_map),
        out_shape=jax.ShapeDtypeStruct((batch * seq, aw), BF16),
        scratch_shapes=[
            pltpu.VMEM((WINDOW + tq, 2 * KV_WIDTH), BF16),
            pltpu.VMEM((WINDOW + tq, 2 * KV_WIDTH), BF16),
            pltpu.VMEM((hb_per_iter, N_KV_HEADS, 2 * KCOLS, LANES), BF16),
            pltpu.VMEM((hb_per_iter, N_KV_HEADS, 2 * KCOLS, 2 * LANES), BF16),
        ],
        compiler_params=_cparams(("arbitrary", "arbitrary")),
        name="attn_meta" if meta_mode else "attn",
    )(q, kk, vv, kk, vv, kkm, vvm, bias, mrow)


POOL_HALO = 16


def _pool_inverse_counts(tm, clip_at_row0):
    t = np.arange(tm, dtype=np.float64)[None, :, None]
    w = np.asarray(POOL_WINDOWS, dtype=np.float64)[:, None, None]
    cnt = np.minimum(t + 1, w) if clip_at_row0 else np.broadcast_to(w, (N_POOL_GROUPS, tm, 1))
    return (1.0 / cnt).astype(np.float32)


def _merge_kernel(*refs, with_router, fc):
    (attn_ref, u_ref, uh_ref, um_ref, g_ref, h_ref, wa_ref, wp_ref, ps_ref, wo_ref, pinv_ref,
     gain_ref) = refs[:12]
    if with_router:
        rt_ref, hout_ref, hn_ref, ri_ref, rw_ref = refs[12:]
    else:
        wg_ref, wu_ref, wd_ref, hout_ref = refs[12:]

    j = pl.program_id(1)
    d = h_ref.shape[1]
    a = jnp.dot(attn_ref[...], wa_ref[...], preferred_element_type=F32)

    before = jnp.where(j == 0, um_ref[...], uh_ref[...])
    x = jnp.concatenate([before, u_ref[...]], axis=0).astype(F32)
    parts = []
    for g, w in enumerate(POOL_WINDOWS):
        xg = x[:, g * POOL_GROUP_DIM:(g + 1) * POOL_GROUP_DIM]
        s, k = xg, 1
        while k < w:
            s = s + pltpu.roll(s, k, 0)
            k *= 2
        pooled = s[POOL_HALO:] * pinv_ref[g] - xg[POOL_HALO:]
        parts.append(jnp.dot(pooled.astype(BF16), wp_ref[g], preferred_element_type=F32))
    pb = jnp.concatenate(parts, axis=1) * ps_ref[...]
    gates = g_ref[...]
    merged = gates[:, :d].astype(F32) * a + gates[:, d:].astype(F32) * pb
    out = jnp.dot(merged.astype(BF16), wo_ref[...], preferred_element_type=F32)
    hnew = h_ref[...] + out
    xn = _rms(hnew, gain_ref[...])
    xh = xn.astype(BF16)
    if not with_router:
        hout_ref[...] = _dense_swiglu(xh, hnew, wg_ref, wu_ref, wd_ref, fc)
    else:
        hout_ref[...] = hnew
        hn_ref[...] = xh
        xl = (xn - xh.astype(F32)).astype(BF16)
        nt = (((1,), (1,)), ((), ()))
        t_hi = lax.dot_general(rt_ref[...], xh, nt, preferred_element_type=F32)
        t_lo = lax.dot_general(rt_ref[...], xl, nt, preferred_element_type=F32)
        lg = t_hi[:N_EXPERTS] + t_hi[N_EXPERTS:] + t_lo[:N_EXPERTS]
        eid = lax.broadcasted_iota(jnp.int32, lg.shape, 0).astype(F32)
        m1 = jnp.max(lg, axis=0, keepdims=True)
        i1 = jnp.min(jnp.where(lg == m1, eid, float(N_EXPERTS)), axis=0, keepdims=True)
        lg2 = jnp.where(eid == i1, -jnp.inf, lg)
        m2 = jnp.max(lg2, axis=0, keepdims=True)
        i2 = jnp.min(jnp.where(lg2 == m2, eid, float(N_EXPERTS)), axis=0, keepdims=True)
        e2 = jnp.exp(m2 - m1)
        w1 = 1.0 / (1.0 + e2)
        w2 = e2 / (1.0 + e2)
        row = lax.broadcasted_iota(jnp.int32, lg.shape, 0)
        ri_ref[...] = jnp.where(row == 0, i1, jnp.where(row == 1, i2, 0.0)).astype(jnp.int32)
        rw_ref[...] = jnp.where(row == 0, w1, jnp.where(row == 1, w2, 0.0))


def _merge(attn, u, u_meta, gates, h, wa, wp, pscale, wo, pool_inv, gain, *, rt=None, ffn=None, fc=None,
           batch, seq, tm):
    t, d = h.shape
    nj = seq // tm
    hs = tm // POOL_HALO
    with_router = rt is not None
    assert with_router != (ffn is not None)

    def row_map(b, j):
        return (b * nj + j, 0)

    def halo_map(b, j):
        return (jnp.maximum((b * nj + j) * hs - 1, 0), 0)

    in_specs = [
        pl.BlockSpec((tm, attn.shape[1]), row_map),
        pl.BlockSpec((tm, POOL_WIDTH), row_map),
        pl.BlockSpec((POOL_HALO, POOL_WIDTH), halo_map),
        _resident((POOL_HALO, POOL_WIDTH)),
        pl.BlockSpec((tm, 2 * d), row_map),
        pl.BlockSpec((tm, d), row_map),
        _resident(wa.shape),
        _resident(wp.shape),
        _resident((1, d)),
        _resident(wo.shape),
        _resident(pool_inv.shape),
        _resident((1, d)),
    ]
    args = [attn, u, u, u_meta, gates, h, wa, wp, pscale, wo, pool_inv, gain]
    out_specs = [pl.BlockSpec((tm, d), row_map)]
    out_shape = [jax.ShapeDtypeStruct((t, d), F32)]
    if with_router:
        in_specs.append(_resident(rt.shape))
        args.append(rt)
        lane_map = lambda b, j: (0, b * nj + j)
        out_specs += [pl.BlockSpec((tm, d), row_map), pl.BlockSpec((8, tm), lane_map), pl.BlockSpec((8, tm), lane_map)]
        out_shape += [jax.ShapeDtypeStruct((t, d), BF16), jax.ShapeDtypeStruct((8, t), jnp.int32),
                      jax.ShapeDtypeStruct((8, t), F32)]
    else:
        in_specs += [_resident(w.shape) for w in ffn]
        args += list(ffn)
    return pl.pallas_call(
        functools.partial(_merge_kernel, with_router=with_router, fc=fc),
        grid=(batch, nj),
        in_specs=in_specs,
        out_specs=out_specs,
        out_shape=out_shape,
        compiler_params=_cparams(("parallel", "parallel")),
        name="merge_router" if with_router else "merge_ffn",
    )(*args)


def _swiglu_act(g, u):
    return (g * jax.nn.sigmoid(g)) * u


def _dense_swiglu(x, residual, wg_ref, wu_ref, wd_ref, fc):
    acc = residual
    for c in range(wg_ref.shape[1] // fc):
        cols = slice(c * fc, (c + 1) * fc)
        g = jnp.dot(x, wg_ref[:, cols], preferred_element_type=F32)
        u = jnp.dot(x, wu_ref[:, cols], preferred_element_type=F32)
        acc = acc + jnp.dot(_swiglu_act(g, u).astype(BF16), wd_ref[cols, :], preferred_element_type=F32)
    return acc


E_ROWS = 16


def _col_to_lanes(col, width):
    r = lax.broadcasted_iota(jnp.int32, (E_ROWS, width), 0)
    c = lax.broadcasted_iota(jnp.int32, (E_ROWS, width), 1)
    return jnp.sum(jnp.where(r == c, col, 0.0), axis=0, keepdims=True)


def _excl_cumsum_col(col):
    r = lax.broadcasted_iota(jnp.int32, (E_ROWS, E_ROWS), 0)
    c = lax.broadcasted_iota(jnp.int32, (E_ROWS, E_ROWS), 1)
    return jnp.sum(jnp.where(c < r, _col_to_lanes(col, E_ROWS), 0.0), axis=1, keepdims=True)


def _route_tab_kernel(ri_ref, rw_ref, tri_ref, slot_ref, tab_ref, fin_ref, carry, *, tn, tmg, nt_pad):
    i = pl.program_id(0)

    @pl.when(i == 0)
    def _():
        carry[...] = jnp.zeros_like(carry)

    eid = lax.broadcasted_iota(jnp.int32, (E_ROWS, tn), 0)
    ri = ri_ref[...]
    oh0 = (eid == ri[0:1, :]).astype(F32)
    oh1 = (eid == ri[1:2, :]).astype(F32)
    oh = oh0 + oh1
    cum = jnp.dot(oh.astype(BF16), tri_ref[...], preferred_element_type=F32)
    cnt = jnp.sum(oh, axis=1, keepdims=True)
    base = _excl_cumsum_col(cnt) + cum
    s0 = jnp.sum(oh0 * base, axis=0, keepdims=True)
    s1 = jnp.sum(oh1 * base, axis=0, keepdims=True)
    row = lax.broadcasted_iota(jnp.int32, (8, tn), 0)
    rw = rw_ref[...]
    slot_ref[...] = jnp.where(row == 0, s0, jnp.where(row == 1, s1, jnp.where(row == 2, rw[0:1, :],
                              jnp.where(row == 3, rw[1:2, :], 0.0))))
    trow = lax.broadcasted_iota(jnp.int32, (8, LANES), 0)
    tab_ref[...] = jnp.where(trow == 0, _col_to_lanes(cnt, LANES),
                             jnp.where(trow == 1, _col_to_lanes(carry[...], LANES), 0.0)).astype(jnp.int32)
    carry[...] += cnt

    @pl.when(i == pl.num_programs(0) - 1)
    def _():
        total = carry[...]
        padded = jnp.ceil(total / tmg) * tmg
        st = _excl_cumsum_col(padded)
        ends = st + padded
        tile0 = (lax.broadcasted_iota(jnp.int32, (E_ROWS, nt_pad), 1) * tmg).astype(F32)
        texp = jnp.minimum(jnp.sum((tile0 >= ends).astype(F32), axis=0, keepdims=True), N_EXPERTS - 1.0)
        nused = jnp.max(ends, axis=0, keepdims=True) / tmg
        frow = lax.broadcasted_iota(jnp.int32, (8, nt_pad), 0)
        fin_ref[...] = jnp.where(frow == 0, texp, jnp.where(frow == 1, nused, jnp.where(
            frow == 2, _col_to_lanes(st, nt_pad), jnp.where(frow == 3, _col_to_lanes(total, nt_pad), 0.0)))
        ).astype(jnp.int32)


def _route_tables(ri, rw, tn, tmg, nt_pad):
    t = ri.shape[1]
    nb = t // tn
    tri = jnp.asarray(np.triu(np.ones((tn, tn), dtype=np.float32), k=1), dtype=BF16)
    lane_blk = pl.BlockSpec((8, tn), lambda i: (0, i))
    return pl.pallas_call(
        functools.partial(_route_tab_kernel, tn=tn, tmg=tmg, nt_pad=nt_pad),
        grid=(nb,),
        in_specs=[lane_blk, lane_blk, pl.BlockSpec((tn, tn), lambda i: (0, 0))],
        out_specs=[lane_blk, pl.BlockSpec((None, 8, LANES), lambda i: (i, 0, 0)),
                   pl.BlockSpec((8, nt_pad), lambda i: (0, 0))],
        out_shape=[jax.ShapeDtypeStruct((8, t), F32), jax.ShapeDtypeStruct((nb, 8, LANES), jnp.int32),
                   jax.ShapeDtypeStruct((8, nt_pad), jnp.int32)],
        scratch_shapes=[pltpu.VMEM((E_ROWS, 1), F32)],
        compiler_params=_cparams(("arbitrary",)),
        name="route_tab",
    )(ri, rw, tri)


ROW_CHUNKS = 8
ROW_PITCH = 9


def _pieces(n, max_log2, fn):
    for b in reversed(range(max_log2 + 1)):
        above = (n >> (b + 1)) << (b + 1)

        @pl.when(((n >> b) & 1) == 1)
        def _():
            fn(above, 1 << b)


def _tile_runs(i, ntab, basetab, gstart, tm, fn):
    off = jnp.int32(0)
    for e in range(N_EXPERTS):
        n_e = ntab[i * E_ROWS + e]
        dst = gstart[e] + basetab[i * E_ROWS + e]
        _pieces(n_e, tm.bit_length() - 1, lambda o, size, off=off, dst=dst: fn(off + o, dst + o, size))
        off = off + n_e


def _rows(ref, start, size):
    return ref.at[pl.ds(start * ROW_PITCH, size * ROW_PITCH)]


def _scatter_kernel(ntab, basetab, gstart, gcount, x_ref, slot_ref, xs_hbm, sbuf, zbuf, sem, zsem, *, tm, tmg):
    i = pl.program_id(0)
    last = pl.num_programs(0) - 1
    n_sorted = xs_hbm.shape[0] // ROW_PITCH
    half = lax.rem(i, 2)
    slots = slot_ref[...]
    rowid = lax.broadcasted_iota(jnp.int32, (2 * tm, tm), 0).astype(F32)
    perm = jnp.where(jnp.logical_or(rowid == slots[0:1, :], rowid == slots[1:2, :]), 1.0, 0.0).astype(BF16)
    srt = jnp.dot(perm, x_ref[...], preferred_element_type=F32)
    @pl.when(i == 0)
    def _():
        sbuf[...] = jnp.zeros_like(sbuf)

    sub0 = half * (2 * tm * ROW_PITCH)
    for c in range(ROW_CHUNKS):
        sbuf[pl.ds(sub0 + c, 2 * tm, stride=ROW_PITCH), :] = srt[:, c * LANES:(c + 1) * LANES]

    def runs(tile, which, act):
        def go(src, dst, size):
            cp = pltpu.make_async_copy(_rows(sbuf, which * (2 * tm) + src, size), _rows(xs_hbm, dst, size),
                                       sem.at[which])
            getattr(cp, act)()
        _tile_runs(tile, ntab, basetab, gstart, tm, go)

    @pl.when(i > 0)
    def _():
        runs(i - 1, 1 - half, "wait")

    runs(i, half, "start")

    @pl.when(i == last)
    def _():
        runs(i, half, "wait")
        zbuf[...] = jnp.zeros_like(zbuf)
        zrows = zbuf.shape[0] // ROW_PITCH

        def zcopy(dst, size):
            return pltpu.make_async_copy(_rows(zbuf, 0, size), _rows(xs_hbm, dst, size), zsem)

        used = gstart[N_EXPERTS]
        for act in ("start", "wait"):
            for e in range(N_EXPERTS):
                end = gstart[e] + gcount[e]
                npad = (tmg - (gcount[e] & (tmg - 1))) & (tmg - 1)
                _pieces(npad, zrows.bit_length() - 1,
                        lambda o, size, end=end, act=act: getattr(zcopy(end + o, size), act)())
            for k in range(N_EXPERTS * tmg // zrows):
                @pl.when(used + k * zrows < n_sorted)
                def _():
                    getattr(zcopy(used + k * zrows, zrows), act)()


def _scatter_rows(ntab, basetab, gstart, gcount, x, slotw, n_sorted, tm, tmg):
    t, d = x.shape
    assert d == ROW_CHUNKS * LANES and tmg & (tmg - 1) == 0 and tm & (tm - 1) == 0
    grid_spec = pltpu.PrefetchScalarGridSpec(
        num_scalar_prefetch=4,
        grid=(t // tm,),
        in_specs=[pl.BlockSpec((tm, d), lambda i, *_: (i, 0)), pl.BlockSpec((8, tm), lambda i, *_: (0, i))],
        out_specs=pl.BlockSpec(memory_space=pl.ANY),
        scratch_shapes=[pltpu.VMEM((2 * 2 * tm * ROW_PITCH, LANES), F32),
                        pltpu.VMEM((tmg // 2 * ROW_PITCH, LANES), F32),
                        pltpu.SemaphoreType.DMA((2,)), pltpu.SemaphoreType.DMA],
    )
    return pl.pallas_call(
        functools.partial(_scatter_kernel, tm=tm, tmg=tmg),
        grid_spec=grid_spec,
        out_shape=jax.ShapeDtypeStruct((n_sorted * ROW_PITCH, LANES), F32),
        compiler_params=_cparams(("arbitrary",)),
        name="moe_scatter",
    )(ntab, basetab, gstart, gcount, x, slotw)


def _grouped_ffn_kernel(te_ref, nu_ref, x_ref, wg_ref, wu_ref, wd_ref, y_ref, xb, acc):
    i = pl.program_id(0)
    j = pl.program_id(1)
    nj = pl.num_programs(1)

    @pl.when(i < nu_ref[0])
    def _():
        tmg = xb.shape[0]

        @pl.when(j == 0)
        def _():
            for c in range(ROW_CHUNKS):
                xb[:, c * LANES:(c + 1) * LANES] = x_ref[pl.ds(c, tmg, stride=ROW_PITCH), :].astype(BF16)
            acc[...] = jnp.zeros_like(acc)

        x = xb[...]
        g = jnp.dot(x, wg_ref[...], preferred_element_type=F32)
        u = jnp.dot(x, wu_ref[...], preferred_element_type=F32)
        acc[...] += jnp.dot(_swiglu_act(g, u).astype(BF16), wd_ref[...], preferred_element_type=F32)

        @pl.when(j == nj - 1)
        def _():
            for c in range(ROW_CHUNKS):
                y_ref[pl.ds(c, tmg, stride=ROW_PITCH), :] = acc[:, c * LANES:(c + 1) * LANES]
            for c in range(ROW_CHUNKS, ROW_PITCH):
                y_ref[pl.ds(c, tmg, stride=ROW_PITCH), :] = jnp.zeros((tmg, LANES), F32)

    @pl.when(jnp.logical_and(i >= nu_ref[0], j == nj - 1))
    def _():
        y_ref[...] = jnp.zeros_like(y_ref)


def _grouped_ffn(te, nused, xs, wg, wu, wd, tmg, fc, nt):
    d, f = wg.shape[1], wg.shape[2]
    nj = f // fc
    blk = tmg * ROW_PITCH

    def tile(i, nu):
        return jnp.minimum(i, nu[0] - 1)

    def chunk(i, j, nu):
        return jnp.where(i < nu[0], j, nj - 1)

    grid_spec = pltpu.PrefetchScalarGridSpec(
        num_scalar_prefetch=2,
        grid=(nt, nj),
        in_specs=[
            pl.BlockSpec((blk, LANES), lambda i, j, te, nu: (tile(i, nu), 0)),
            pl.BlockSpec((None, d, fc), lambda i, j, te, nu: (te[tile(i, nu)], 0, chunk(i, j, nu))),
            pl.BlockSpec((None, d, fc), lambda i, j, te, nu: (te[tile(i, nu)], 0, chunk(i, j, nu))),
            pl.BlockSpec((None, fc, d), lambda i, j, te, nu: (te[tile(i, nu)], chunk(i, j, nu), 0)),
        ],
        out_specs=pl.BlockSpec((blk, LANES), lambda i, j, te, nu: (i, 0)),
        scratch_shapes=[pltpu.VMEM((tmg, d), BF16), pltpu.VMEM((tmg, d), F32)],
    )
    return pl.pallas_call(
        _grouped_ffn_kernel,
        grid_spec=grid_spec,
        out_shape=jax.ShapeDtypeStruct(xs.shape, F32),
        compiler_params=_cparams(("arbitrary", "arbitrary")),
        name="moe_ffn",
    )(te, nused, xs, wg, wu, wd)


def _combine_kernel(ntab, basetab, gstart, ys_hbm, slot_ref, h_ref, gain_ref, o_ref, ybuf, yg, sem, *, tm):
    i = pl.program_id(0)
    half = lax.rem(i, 2)

    def runs(tile, which, act):
        def go(loc, src, size):
            cp = pltpu.make_async_copy(_rows(ys_hbm, src, size), _rows(ybuf, which * (2 * tm) + loc, size),
                                       sem.at[which])
            getattr(cp, act)()
        _tile_runs(tile, ntab, basetab, gstart, tm, go)

    @pl.when(i == 0)
    def _():
        runs(i, half, "start")

    @pl.when(i + 1 < pl.num_programs(0))
    def _():
        runs(i + 1, 1 - half, "start")

    runs(i, half, "wait")
    sub0 = half * (2 * tm * ROW_PITCH)
    for c in range(ROW_CHUNKS):
        yg[:, c * LANES:(c + 1) * LANES] = ybuf[pl.ds(sub0 + c, 2 * tm, stride=ROW_PITCH), :].astype(BF16)

    sw = jnp.concatenate([slot_ref[...], jnp.zeros((LANES - 8, tm), F32)], axis=0).T
    lane = lax.broadcasted_iota(jnp.int32, (tm, 2 * tm), 1).astype(F32)
    pick = jnp.where(lane == sw[:, 0:1], sw[:, 2:3], jnp.where(lane == sw[:, 1:2], sw[:, 3:4], 0.0))
    y = h_ref[...] + jnp.dot(pick.astype(BF16), yg[...], preferred_element_type=F32)
    o_ref[...] = _rms(y, gain_ref[...])


def _combine(ntab, basetab, gstart, ys, slotw, h, gain, tm):
    t, d = h.shape
    grid_spec = pltpu.PrefetchScalarGridSpec(
        num_scalar_prefetch=3,
        grid=(t // tm,),
        in_specs=[pl.BlockSpec(memory_space=pl.ANY),
                  pl.BlockSpec((8, tm), lambda i, *_: (0, i)),
                  pl.BlockSpec((tm, d), lambda i, *_: (i, 0)),
                  pl.BlockSpec((1, d), lambda i, *_: (0, 0))],
        out_specs=pl.BlockSpec((tm, d), lambda i, *_: (i, 0)),
        scratch_shapes=[pltpu.VMEM((2 * 2 * tm * ROW_PITCH, LANES), F32), pltpu.VMEM((2 * tm, d), BF16),
                        pltpu.SemaphoreType.DMA((2,))],
    )
    return pl.pallas_call(
        functools.partial(_combine_kernel, tm=tm),
        grid_spec=grid_spec,
        out_shape=jax.ShapeDtypeStruct((t, d), F32),
        compiler_params=_cparams(("arbitrary",)),
        name="moe_combine",
    )(ntab, basetab, gstart, ys, slotw, h, gain)


def _half_if_aligned(n):
    return n // 2 if n % (2 * LANES) == 0 else n


def _tiles(seq, d_ff, d_ff_expert):
    return dict(
        tm_in=1024,
        tq=min(1024, seq),
        tm_mg=512,
        fc_ffn=_half_if_aligned(d_ff),
        tm_sc=512,
        tmg=512, fc_moe=_half_if_aligned(d_ff_expert),
    )


def kernel(x, meta_tokens, norm_mix, w_in, attn_sinks, w_attn_br, w_pool_grp, pool_scale, w_out, norm_ffn,
           dense_w_gate, dense_w_up, dense_w_down, moe_router, moe_w_gate, moe_w_up, moe_w_down, norm_final):
    batch, seq, d = x.shape
    depth = w_in.shape[0]
    assert depth == 2 and dense_w_gate.shape[0] == 1 and moe_router.shape[0] == 1, "dense layer then expert layer"
    assert moe_router.shape[2] == N_EXPERTS and w_in.shape[2] == N_HEADS * HEAD_DIM + 2 * KV_WIDTH + POOL_WIDTH + 2 * d
    t = batch * seq
    cfg = _tiles(seq, dense_w_gate.shape[2], moe_w_gate.shape[3])
    bf = lambda a: a.astype(BF16)

    bias_np, mrow_np, sinkpos_np = _attn_bias_tables()
    mrow = jnp.asarray(mrow_np)
    pool_inv_main = jnp.asarray(_pool_inverse_counts(cfg["tm_mg"], False))
    pool_inv_meta = jnp.asarray(_pool_inverse_counts(META_ROWS, True))

    h = x.reshape(t, d)
    hm = jnp.concatenate([meta_tokens.astype(F32), jnp.zeros((META_ROWS - N_META, d), F32)], axis=0)

    out = None
    for layer in range(depth):
        gain = norm_mix[layer].reshape(1, d)
        w_in_bf = bf(w_in[layer])
        sink_row = jnp.repeat(attn_sinks[layer].astype(F32) * LOG2E, KCOLS).reshape(N_HEADS // 2, 1, 2 * KCOLS)
        bias = jnp.asarray(bias_np) + (jnp.asarray(sinkpos_np) * sink_row)[None]
        wa, wo = bf(w_attn_br[layer]), bf(w_out[layer])
        wp = bf(w_pool_grp[layer])
        ps = pool_scale[layer].reshape(1, d)
        gain_ffn = norm_ffn[layer].reshape(1, d)

        qm, kkm_all, vvm_all, um, gm = _inproj(hm, gain, w_in_bf, META_ROWS)
        kkm, vvm = kkm_all[:N_META], vvm_all[:N_META]
        um16 = um[:N_META]
        q, kk, vv, u, g = _inproj(h, gain, w_in_bf, cfg["tm_in"])
        attn = _attention(q, kk, vv, kkm, vvm, bias, mrow, batch=batch, seq=seq, tq=cfg["tq"], meta_mode=False)

        u_before = um16
        if layer == 0:
            attn_m = _attention(qm, kkm_all, vvm_all, kkm, vvm, bias, mrow,
                                batch=1, seq=META_ROWS, tq=META_ROWS, meta_mode=True)
            ffn = (bf(dense_w_gate[0]), bf(dense_w_up[0]), bf(dense_w_down[0]))
            hm, = _merge(attn_m, um, jnp.zeros_like(u_before), gm, hm, wa, wp, ps, wo, pool_inv_meta, gain_ffn,
                         ffn=ffn, fc=cfg["fc_ffn"], batch=1, seq=META_ROWS, tm=META_ROWS)
            h, = _merge(attn, u, u_before, g, h, wa, wp, ps, wo, pool_inv_main, gain_ffn,
                        ffn=ffn, fc=cfg["fc_ffn"], batch=batch, seq=seq, tm=cfg["tm_mg"])
        else:
            r = moe_router[0].astype(F32)
            r_hi = r.astype(BF16)
            r_lo = (r - r_hi.astype(F32)).astype(BF16)
            rt = jnp.concatenate([r_hi.T, r_lo.T], axis=0)
            h, hn, ri, rw = _merge(attn, u, u_before, g, h, wa, wp, ps, wo, pool_inv_main, gain_ffn, rt=rt,
                                   batch=batch, seq=seq, tm=cfg["tm_mg"])
            wg, wu, wd = bf(moe_w_gate[0]), bf(moe_w_up[0]), bf(moe_w_down[0])
            out = _moe(h, hn, ri, rw, wg, wu, wd, norm_final.reshape(1, d), cfg)
    return out.reshape(batch, seq, d)


def _moe(h, hn, ri, rw, wg, wu, wd, gain_final, cfg):
    t, d = h.shape
    tmg, tm = cfg["tmg"], cfg["tm_sc"]
    nt = 2 * t // tmg + N_EXPERTS
    nt_pad = -(-nt // LANES) * LANES
    slotw, tab, fin = _route_tables(ri, rw, tm, tmg, nt_pad)
    ntab = tab[:, 0, :E_ROWS].reshape(-1)
    basetab = tab[:, 1, :E_ROWS].reshape(-1)
    te, nused, gstart, gcount = fin[0, :nt], fin[1, :1], fin[2, :E_ROWS], fin[3, :E_ROWS]
    xs = _scatter_rows(ntab, basetab, gstart, gcount, hn, slotw, nt * tmg, tm, tmg)
    ys = _grouped_ffn(te, nused, xs, wg, wu, wd, tmg, cfg["fc_moe"], nt)
    return _combine(ntab, basetab, gstart, ys, slotw, h, gain_final, tm)
```

```python
import functools

import numpy as np
import jax
import jax.numpy as jnp
from jax import lax
from jax.experimental import pallas as pl
from jax.experimental.pallas import tpu as pltpu

F32 = jnp.float32
BF16 = jnp.bfloat16

N_HEADS = 16
HEAD_DIM = 64
N_KV_HEADS = 2
KV_WIDTH = N_KV_HEADS * HEAD_DIM
WINDOW = 128
N_META = 16
POOL_WINDOWS = (2, 4, 8, 16)
N_POOL_GROUPS = 4
POOL_GROUP_DIM = 128
POOL_WIDTH = N_POOL_GROUPS * POOL_GROUP_DIM
N_EXPERTS = 8
RMS_EPS = 1e-5
NEG_BIAS = -1e30
LOG2E = 1.4426950408889634

LANES = 128
SUBLANES = 8
META_ROWS = 128
VMEM_LIMIT = 56 * 1024 * 1024
GATE_COLS = 512

QB = 64
KB = WINDOW + QB
KCOLS = 256
SINK_COL = KB + N_META


def _cparams(sem, vmem=VMEM_LIMIT):
    return pltpu.CompilerParams(dimension_semantics=sem, vmem_limit_bytes=vmem)


def _resident(shape):
    zeros = (0,) * len(shape)
    return pl.BlockSpec(shape, lambda *_: zeros, pipeline_mode=pl.Buffered(1))


def _rms(x, gain):
    ms = jnp.mean(x * x, axis=-1, keepdims=True)
    return x * lax.rsqrt(ms + RMS_EPS) * gain


def _inproj_kernel(h_ref, gain_ref, w_ref, q_ref, kk_ref, vv_ref, u_ref, g_ref, *, d_model):
    xn = _rms(h_ref[...], gain_ref[...]).astype(BF16)
    aw = N_HEADS * HEAD_DIM
    qscale = (HEAD_DIM ** -0.5) * LOG2E
    half = aw // 2
    for c in range(2):
        q = jnp.dot(xn, w_ref[:, c * half:(c + 1) * half], preferred_element_type=F32)
        q_ref[:, c * half:(c + 1) * half] = (q * qscale).astype(BF16)
    kv = jnp.dot(xn, w_ref[:, aw:aw + 2 * KV_WIDTH], preferred_element_type=F32)
    k = kv[:, :KV_WIDTH]
    v = kv[:, KV_WIDTH:]
    kk_ref[:, :KV_WIDTH] = k.astype(BF16)
    kk_ref[:, KV_WIDTH:] = pltpu.roll(k, HEAD_DIM, 1).astype(BF16)
    vv_ref[:, :KV_WIDTH] = v.astype(BF16)
    vv_ref[:, KV_WIDTH:] = pltpu.roll(v, HEAD_DIM, 1).astype(BF16)
    o = aw + 2 * KV_WIDTH
    u_ref[...] = jnp.dot(xn, w_ref[:, o:o + POOL_WIDTH], preferred_element_type=F32).astype(BF16)
    o += POOL_WIDTH
    for c in range(2 * d_model // GATE_COLS):
        cols = slice(c * GATE_COLS, (c + 1) * GATE_COLS)
        g = jnp.dot(xn, w_ref[:, o + cols.start:o + cols.stop], preferred_element_type=F32)
        g_ref[:, cols] = jax.nn.sigmoid(g).astype(BF16)


def _inproj(h, gain, w_bf, tm):
    t, d = h.shape
    n = w_bf.shape[1]
    aw = N_HEADS * HEAD_DIM
    row = lambda w: pl.BlockSpec((tm, w), lambda i: (i, 0))
    return pl.pallas_call(
        functools.partial(_inproj_kernel, d_model=d),
        grid=(t // tm,),
        in_specs=[row(d), _resident((1, d)), _resident((d, n))],
        out_specs=[row(aw), row(2 * KV_WIDTH), row(2 * KV_WIDTH), row(POOL_WIDTH), row(2 * d)],
        out_shape=[
            jax.ShapeDtypeStruct((t, aw), BF16),
            jax.ShapeDtypeStruct((t, 2 * KV_WIDTH), BF16),
            jax.ShapeDtypeStruct((t, 2 * KV_WIDTH), BF16),
            jax.ShapeDtypeStruct((t, POOL_WIDTH), BF16),
            jax.ShapeDtypeStruct((t, 2 * d), BF16),
        ],
        compiler_params=_cparams(("parallel",)),
        name="inproj",
    )(h, gain, w_bf)


def _attn_bias_tables():
    slopes = np.array([2.0 ** (-8.0 * (h + 1) / N_HEADS) for h in range(N_HEADS)], dtype=np.float64)
    i = np.arange(QB)[:, None]
    c = np.arange(KCOLS)[None, :]
    d_band = WINDOW + i - c
    ok_band = (c < KB) & (d_band >= 0) & (d_band < WINDOW)
    m = c - KB
    is_meta = (c >= KB) & (c < KB + N_META)
    d_meta = N_META + i - m
    tbl = np.full((4, N_HEADS, QB, KCOLS), NEG_BIAS, dtype=np.float64)
    for var in range(4):
        if var == 0:
            okb = ok_band
        elif var == 1:
            okb = ok_band & (c >= WINDOW)
        elif var == 2:
            okb = ok_band & (c >= WINDOW - QB)
        else:
            okb = np.zeros_like(ok_band)
        okm = np.broadcast_to(is_meta, (QB, KCOLS)) & ((d_meta - N_META >= 0) if var == 3 else True)
        okb = np.broadcast_to(okb, (QB, KCOLS))
        for h in range(N_HEADS):
            t = tbl[var, h]
            t[okb] = (-slopes[h] * LOG2E * d_band)[okb]
            t[okm] = (-slopes[h] * LOG2E * d_meta)[okm]
            t[:, SINK_COL] = 0.0
    tbl = tbl.reshape(4, N_HEADS // 2, 2, QB, KCOLS).transpose(0, 1, 3, 2, 4)
    tbl = tbl.reshape(4, N_HEADS // 2, QB, 2 * KCOLS)
    mrow = np.zeros((N_HEADS, KCOLS), dtype=np.float64)
    mrow[:, KB:KB + N_META] = (-slopes * LOG2E)[:, None]
    mrow = mrow.reshape(N_HEADS // 2, 1, 2 * KCOLS)
    sink_pos = np.zeros((N_HEADS // 2, 1, 2 * KCOLS), dtype=np.float32)
    sink_pos[:, 0, SINK_COL] = 1.0
    sink_pos[:, 0, KCOLS + SINK_COL] = 1.0
    return tbl.astype(np.float32), mrow.astype(np.float32), sink_pos


def _attn_kernel(q_ref, kk_ref, vv_ref, kkh_ref, vvh_ref, kkm_ref, vvm_ref, bias_ref, mrow_ref,
                 o_ref, khead, vhead, bmat, vmat, *, tq, meta_mode):
    n_hb = bmat.shape[0]
    j = pl.program_id(1)
    hd = HEAD_DIM

    khead[0:WINDOW, :] = kkh_ref[...]
    khead[WINDOW:, :] = kk_ref[0:WINDOW, :]
    vhead[0:WINDOW, :] = vvh_ref[...]
    vhead[WINDOW:, :] = vv_ref[0:WINDOW, :]

    def place(dst, par, src, rows, kvh, row0, nrows):
        lo_col = 0 if kvh == 0 else 2 * hd
        hi_col = 3 * hd if kvh == 0 else hd
        dst[par, kvh, row0:row0 + nrows, 0:hd] = src[rows, lo_col:lo_col + hd]
        dst[par, kvh, KCOLS + row0:KCOLS + row0 + nrows, hd:2 * hd] = src[rows, hi_col:hi_col + hd]

    @pl.when(jnp.logical_and(pl.program_id(0) == 0, j == 0))
    def _():
        bmat[...] = jnp.zeros_like(bmat)
        for par in range(n_hb):
            for kvh in range(N_KV_HEADS):
                vmat[par, kvh, :, 0:LANES] = jnp.zeros((2 * KCOLS, LANES), BF16)
                vmat[par, kvh, 0:KCOLS, LANES:LANES + hd] = jnp.ones((KCOLS, hd), BF16)
                vmat[par, kvh, 0:KCOLS, LANES + hd:2 * LANES] = jnp.zeros((KCOLS, hd), BF16)
                vmat[par, kvh, KCOLS:2 * KCOLS, LANES:LANES + hd] = jnp.zeros((KCOLS, hd), BF16)
                vmat[par, kvh, KCOLS:2 * KCOLS, LANES + hd:2 * LANES] = jnp.ones((KCOLS, hd), BF16)
                place(bmat, par, kkm_ref, slice(None), kvh, KB, N_META)
                place(vmat, par, vvm_ref, slice(None), kvh, KB, N_META)

    ppk = N_HEADS // 2 // N_KV_HEADS

    for s in range(n_hb):
        o = s * QB
        if meta_mode:
            var = 3
            pos0 = float(o - N_META)
        else:
            var = jnp.where(j == 0, s + 1, 0) if s < 2 else 0
            pos0 = (j * tq + o).astype(F32)
        band = slice(o, o + KB) if s < 2 else slice(o - WINDOW, o + QB)
        ksrc, vsrc = (khead, vhead) if s < 2 else (kk_ref, vv_ref)
        for kvh in range(N_KV_HEADS):
            place(bmat, s, ksrc, band, kvh, 0, KB)
            place(vmat, s, vsrc, band, kvh, 0, KB)
        for kvh in range(N_KV_HEADS):
            pairs = range(kvh * ppk, (kvh + 1) * ppk)
            qs = jnp.concatenate([q_ref[o:o + QB, p * LANES:(p + 1) * LANES] for p in pairs], axis=0)
            sc = lax.dot_general(qs, bmat[s, kvh], (((1,), (1,)), ((), ())), preferred_element_type=F32)
            probs = []
            for n, p in enumerate(pairs):
                sp = sc[n * QB:(n + 1) * QB, :] + (bias_ref[var, p] + mrow_ref[p] * pos0)
                m0 = jnp.max(sp[:, :KCOLS], axis=1, keepdims=True)
                m1 = jnp.max(sp[:, KCOLS:], axis=1, keepdims=True)
                pr = jnp.concatenate([jnp.exp2(sp[:, :KCOLS] - m0), jnp.exp2(sp[:, KCOLS:] - m1)], axis=1)
                probs.append(pr.astype(BF16))
            ov = jnp.dot(jnp.concatenate(probs, axis=0), vmat[s, kvh], preferred_element_type=F32)
            for n, p in enumerate(pairs):
                on = ov[n * QB:(n + 1) * QB, :]
                o_ref[o:o + QB, p * LANES:(p + 1) * LANES] = (on[:, :LANES] / on[:, LANES:]).astype(BF16)


def _attention(q, kk, vv, kkm, vvm, bias, mrow, *, batch, seq, tq, meta_mode):
    aw = N_HEADS * HEAD_DIM
    nj = seq // tq
    hb = tq // WINDOW if not meta_mode else 1
    n_hb = tq // QB
    assert tq % WINDOW == 0

    def row_map(b, j):
        return (b * nj + j, 0)

    def halo_map(b, j):
        if meta_mode:
            return (0, 0)
        return (jnp.maximum(b * nj * hb + j * hb - 1, b * nj * hb), 0)

    const2 = lambda b, j: (0, 0)
    return pl.pallas_call(
        functools.partial(_attn_kernel, tq=tq, meta_mode=meta_mode),
        grid=(batch, nj),
        in_specs=[
            pl.BlockSpec((tq, aw), row_map),
            pl.BlockSpec((tq, 2 * KV_WIDTH), row_map),
            pl.BlockSpec((tq, 2 * KV_WIDTH), row_map),
            pl.BlockSpec((WINDOW, 2 * KV_WIDTH), halo_map),
            pl.BlockSpec((WINDOW, 2 * KV_WIDTH), halo_map),
            pl.BlockSpec((N_META, 2 * KV_WIDTH), const2),
            pl.BlockSpec((N_META, 2 * KV_WIDTH), const2),
            pl.BlockSpec(bias.shape, lambda b, j: (0, 0, 0, 0)),
            pl.BlockSpec(mrow.shape, lambda b, j: (0, 0, 0)),
        ],
        out_specs=pl.BlockSpec((tq, aw), row_map),
        out_shape=jax.ShapeDtypeStruct((batch * seq, aw), BF16),
        scratch_shapes=[
            pltpu.VMEM((2 * WINDOW, 2 * KV_WIDTH), BF16),
            pltpu.VMEM((2 * WINDOW, 2 * KV_WIDTH), BF16),
            pltpu.VMEM((n_hb, N_KV_HEADS, 2 * KCOLS, LANES), BF16),
            pltpu.VMEM((n_hb, N_KV_HEADS, 2 * KCOLS, 2 * LANES), BF16),
        ],
        compiler_params=_cparams(("arbitrary", "arbitrary")),
        name="attn_meta" if meta_mode else "attn",
    )(q, kk, vv, kk, vv, kkm, vvm, bias, mrow)


POOL_HALO = 16


def _pool_inverse_counts(tm, clip_at_row0):
    t = np.arange(tm, dtype=np.float64)[None, :, None]
    w = np.asarray(POOL_WINDOWS, dtype=np.float64)[:, None, None]
    cnt = np.minimum(t + 1, w) if clip_at_row0 else np.broadcast_to(w, (N_POOL_GROUPS, tm, 1))
    return (1.0 / cnt).astype(np.float32)


def _merge_kernel(*refs, with_router, fc):
    (attn_ref, u_ref, uh_ref, um_ref, g_ref, h_ref, wa_ref, wp_ref, ps_ref, wo_ref, pinv_ref,
     gain_ref) = refs[:12]
    if with_router:
        rt_ref, hout_ref, hn_ref, ri_ref, rw_ref = refs[12:]
    else:
        wg_ref, wu_ref, wd_ref, hout_ref = refs[12:]

    j = pl.program_id(1)
    d = h_ref.shape[1]
    a = jnp.dot(attn_ref[...], wa_ref[...], preferred_element_type=F32)

    before = jnp.where(j == 0, um_ref[...], uh_ref[...])
    x = jnp.concatenate([before, u_ref[...]], axis=0).astype(F32)
    parts = []
    for g, w in enumerate(POOL_WINDOWS):
        xg = x[:, g * POOL_GROUP_DIM:(g + 1) * POOL_GROUP_DIM]
        s, k = xg, 1
        while k < w:
            s = s + pltpu.roll(s, k, 0)
            k *= 2
        pooled = s[POOL_HALO:] * pinv_ref[g] - xg[POOL_HALO:]
        parts.append(jnp.dot(pooled.astype(BF16), wp_ref[g], preferred_element_type=F32))
    pb = jnp.concatenate(parts, axis=1) * ps_ref[...]
    gates = g_ref[...]
    merged = gates[:, :d].astype(F32) * a + gates[:, d:].astype(F32) * pb
    out = jnp.dot(merged.astype(BF16), wo_ref[...], preferred_element_type=F32)
    hnew = h_ref[...] + out
    xn = _rms(hnew, gain_ref[...])
    xh = xn.astype(BF16)
    if not with_router:
        hout_ref[...] = _dense_swiglu(xh, hnew, wg_ref, wu_ref, wd_ref, fc)
    else:
        hout_ref[...] = hnew
        hn_ref[...] = xh
        xl = (xn - xh.astype(F32)).astype(BF16)
        nt = (((1,), (1,)), ((), ()))
        t_hi = lax.dot_general(rt_ref[...], xh, nt, preferred_element_type=F32)
        t_lo = lax.dot_general(rt_ref[...], xl, nt, preferred_element_type=F32)
        lg = t_hi[:N_EXPERTS] + t_hi[N_EXPERTS:] + t_lo[:N_EXPERTS]
        eid = lax.broadcasted_iota(jnp.int32, lg.shape, 0).astype(F32)
        m1 = jnp.max(lg, axis=0, keepdims=True)
        i1 = jnp.min(jnp.where(lg == m1, eid, float(N_EXPERTS)), axis=0, keepdims=True)
        lg2 = jnp.where(eid == i1, -jnp.inf, lg)
        m2 = jnp.max(lg2, axis=0, keepdims=True)
        i2 = jnp.min(jnp.where(lg2 == m2, eid, float(N_EXPERTS)), axis=0, keepdims=True)
        e2 = jnp.exp(m2 - m1)
        w1 = 1.0 / (1.0 + e2)
        w2 = e2 / (1.0 + e2)
        row = lax.broadcasted_iota(jnp.int32, lg.shape, 0)
        ri_ref[...] = jnp.where(row == 0, i1, jnp.where(row == 1, i2, 0.0)).astype(jnp.int32)
        rw_ref[...] = jnp.where(row == 0, w1, jnp.where(row == 1, w2, 0.0))


def _merge(attn, u, u_meta, gates, h, wa, wp, pscale, wo, pool_inv, gain, *, rt=None, ffn=None, fc=None,
           batch, seq, tm):
    t, d = h.shape
    nj = seq // tm
    hs = tm // POOL_HALO
    with_router = rt is not None
    assert with_router != (ffn is not None)

    def row_map(b, j):
        return (b * nj + j, 0)

    def halo_map(b, j):
        return (jnp.maximum((b * nj + j) * hs - 1, 0), 0)

    in_specs = [
        pl.BlockSpec((tm, attn.shape[1]), row_map),
        pl.BlockSpec((tm, POOL_WIDTH), row_map),
        pl.BlockSpec((POOL_HALO, POOL_WIDTH), halo_map),
        _resident((POOL_HALO, POOL_WIDTH)),
        pl.BlockSpec((tm, 2 * d), row_map),
        pl.BlockSpec((tm, d), row_map),
        _resident(wa.shape),
        _resident(wp.shape),
        _resident((1, d)),
        _resident(wo.shape),
        _resident(pool_inv.shape),
        _resident((1, d)),
    ]
    args = [attn, u, u, u_meta, gates, h, wa, wp, pscale, wo, pool_inv, gain]
    out_specs = [pl.BlockSpec((tm, d), row_map)]
    out_shape = [jax.ShapeDtypeStruct((t, d), F32)]
    if with_router:
        in_specs.append(_resident(rt.shape))
        args.append(rt)
        lane_map = lambda b, j: (0, b * nj + j)
        out_specs += [pl.BlockSpec((tm, d), row_map), pl.BlockSpec((SUBLANES, tm), lane_map),
                      pl.BlockSpec((SUBLANES, tm), lane_map)]
        out_shape += [jax.ShapeDtypeStruct((t, d), BF16), jax.ShapeDtypeStruct((SUBLANES, t), jnp.int32),
                      jax.ShapeDtypeStruct((SUBLANES, t), F32)]
    else:
        in_specs += [_resident(w.shape) for w in ffn]
        args += list(ffn)
    return pl.pallas_call(
        functools.partial(_merge_kernel, with_router=with_router, fc=fc),
        grid=(batch, nj),
        in_specs=in_specs,
        out_specs=out_specs,
        out_shape=out_shape,
        compiler_params=_cparams(("parallel", "parallel")),
        name="merge_router" if with_router else "merge_ffn",
    )(*args)


def _swiglu_act(g, u):
    return (g * jax.nn.sigmoid(g)) * u


def _dense_swiglu(x, residual, wg_ref, wu_ref, wd_ref, fc):
    acc = residual
    for c in range(wg_ref.shape[1] // fc):
        cols = slice(c * fc, (c + 1) * fc)
        g = jnp.dot(x, wg_ref[:, cols], preferred_element_type=F32)
        u = jnp.dot(x, wu_ref[:, cols], preferred_element_type=F32)
        acc = acc + jnp.dot(_swiglu_act(g, u).astype(BF16), wd_ref[cols, :], preferred_element_type=F32)
    return acc


E_ROWS = 16


def _col_to_lanes(col, width):
    r = lax.broadcasted_iota(jnp.int32, (E_ROWS, width), 0)
    c = lax.broadcasted_iota(jnp.int32, (E_ROWS, width), 1)
    return jnp.sum(jnp.where(r == c, col, 0.0), axis=0, keepdims=True)


def _excl_cumsum_col(col):
    r = lax.broadcasted_iota(jnp.int32, (E_ROWS, E_ROWS), 0)
    c = lax.broadcasted_iota(jnp.int32, (E_ROWS, E_ROWS), 1)
    return jnp.sum(jnp.where(c < r, _col_to_lanes(col, E_ROWS), 0.0), axis=1, keepdims=True)


def _route_tab_kernel(ri_ref, rw_ref, tri_ref, slot_ref, tab_ref, fin_ref, carry, *, tn, tmg, nt_pad):
    i = pl.program_id(0)

    @pl.when(i == 0)
    def _():
        carry[...] = jnp.zeros_like(carry)

    eid = lax.broadcasted_iota(jnp.int32, (E_ROWS, tn), 0)
    ri = ri_ref[...]
    oh0 = (eid == ri[0:1, :]).astype(F32)
    oh1 = (eid == ri[1:2, :]).astype(F32)
    oh = oh0 + oh1
    cum = jnp.dot(oh.astype(BF16), tri_ref[...], preferred_element_type=F32)
    cnt = jnp.sum(oh, axis=1, keepdims=True)
    base = _excl_cumsum_col(cnt) + cum
    s0 = jnp.sum(oh0 * base, axis=0, keepdims=True)
    s1 = jnp.sum(oh1 * base, axis=0, keepdims=True)
    row = lax.broadcasted_iota(jnp.int32, (SUBLANES, tn), 0)
    rw = rw_ref[...]
    slot_ref[...] = jnp.where(row == 0, s0, jnp.where(row == 1, s1, jnp.where(row == 2, rw[0:1, :],
                              jnp.where(row == 3, rw[1:2, :], 0.0))))
    trow = lax.broadcasted_iota(jnp.int32, (SUBLANES, LANES), 0)
    tab_ref[...] = jnp.where(trow == 0, _col_to_lanes(cnt, LANES),
                             jnp.where(trow == 1, _col_to_lanes(carry[...], LANES), 0.0)).astype(jnp.int32)
    carry[...] += cnt

    @pl.when(i == pl.num_programs(0) - 1)
    def _():
        total = carry[...]
        padded = jnp.ceil(total / tmg) * tmg
        st = _excl_cumsum_col(padded)
        ends = st + padded
        tile0 = (lax.broadcasted_iota(jnp.int32, (E_ROWS, nt_pad), 1) * tmg).astype(F32)
        texp = jnp.minimum(jnp.sum((tile0 >= ends).astype(F32), axis=0, keepdims=True), N_EXPERTS - 1.0)
        nused = jnp.max(ends, axis=0, keepdims=True) / tmg
        frow = lax.broadcasted_iota(jnp.int32, (SUBLANES, nt_pad), 0)
        fin_ref[...] = jnp.where(frow == 0, texp, jnp.where(frow == 1, nused, jnp.where(
            frow == 2, _col_to_lanes(st, nt_pad), jnp.where(frow == 3, _col_to_lanes(total, nt_pad), 0.0)))
        ).astype(jnp.int32)


def _route_tables(ri, rw, tn, tmg, nt_pad):
    t = ri.shape[1]
    nb = t // tn
    tri = jnp.asarray(np.triu(np.ones((tn, tn), dtype=np.float32), k=1), dtype=BF16)
    lane_blk = pl.BlockSpec((SUBLANES, tn), lambda i: (0, i))
    return pl.pallas_call(
        functools.partial(_route_tab_kernel, tn=tn, tmg=tmg, nt_pad=nt_pad),
        grid=(nb,),
        in_specs=[lane_blk, lane_blk, pl.BlockSpec((tn, tn), lambda i: (0, 0))],
        out_specs=[lane_blk, pl.BlockSpec((None, SUBLANES, LANES), lambda i: (i, 0, 0)),
                   pl.BlockSpec((SUBLANES, nt_pad), lambda i: (0, 0))],
        out_shape=[jax.ShapeDtypeStruct((SUBLANES, t), F32), jax.ShapeDtypeStruct((nb, SUBLANES, LANES), jnp.int32),
                   jax.ShapeDtypeStruct((SUBLANES, nt_pad), jnp.int32)],
        scratch_shapes=[pltpu.VMEM((E_ROWS, 1), F32)],
        compiler_params=_cparams(("arbitrary",)),
        name="route_tab",
    )(ri, rw, tri)


ROW_CHUNKS = 8
ROW_PITCH = 9


def _pieces(n, max_log2, fn):
    for b in reversed(range(max_log2 + 1)):
        above = (n >> (b + 1)) << (b + 1)

        @pl.when(((n >> b) & 1) == 1)
        def _():
            fn(above, 1 << b)


def _tile_runs(i, ntab, basetab, gstart, tm, fn):
    off = jnp.int32(0)
    for e in range(N_EXPERTS):
        n_e = ntab[i * E_ROWS + e]
        dst = gstart[e] + basetab[i * E_ROWS + e]
        _pieces(n_e, tm.bit_length() - 1, lambda o, size, off=off, dst=dst: fn(off + o, dst + o, size))
        off = off + n_e


def _rows(ref, start, size):
    return ref.at[pl.ds(start * ROW_PITCH, size * ROW_PITCH)]


def _scatter_kernel(ntab, basetab, gstart, gcount, x_ref, slot_ref, xs_hbm, sbuf, zbuf, sem, zsem, *, tm, tmg):
    i = pl.program_id(0)
    last = pl.num_programs(0) - 1
    n_sorted = xs_hbm.shape[0] // ROW_PITCH
    half = lax.rem(i, 2)
    slots = slot_ref[...]
    rowid = lax.broadcasted_iota(jnp.int32, (2 * tm, tm), 0).astype(F32)
    perm = jnp.where(jnp.logical_or(rowid == slots[0:1, :], rowid == slots[1:2, :]), 1.0, 0.0).astype(BF16)
    srt = jnp.dot(perm, x_ref[...], preferred_element_type=F32)
    @pl.when(i == 0)
    def _():
        sbuf[...] = jnp.zeros_like(sbuf)

    sub0 = half * (2 * tm * ROW_PITCH)
    for c in range(ROW_CHUNKS):
        sbuf[pl.ds(sub0 + c, 2 * tm, stride=ROW_PITCH), :] = srt[:, c * LANES:(c + 1) * LANES]

    def runs(tile, which, act):
        def go(src, dst, size):
            cp = pltpu.make_async_copy(_rows(sbuf, which * (2 * tm) + src, size), _rows(xs_hbm, dst, size),
                                       sem.at[which])
            getattr(cp, act)()
        _tile_runs(tile, ntab, basetab, gstart, tm, go)

    @pl.when(i > 0)
    def _():
        runs(i - 1, 1 - half, "wait")

    runs(i, half, "start")

    @pl.when(i == last)
    def _():
        runs(i, half, "wait")
        zbuf[...] = jnp.zeros_like(zbuf)
        zrows = zbuf.shape[0] // ROW_PITCH

        def zcopy(dst, size):
            return pltpu.make_async_copy(_rows(zbuf, 0, size), _rows(xs_hbm, dst, size), zsem)

        used = gstart[N_EXPERTS]
        for act in ("start", "wait"):
            for e in range(N_EXPERTS):
                end = gstart[e] + gcount[e]
                npad = (tmg - (gcount[e] & (tmg - 1))) & (tmg - 1)
                _pieces(npad, zrows.bit_length() - 1,
                        lambda o, size, end=end, act=act: getattr(zcopy(end + o, size), act)())
            for k in range(N_EXPERTS * tmg // zrows):
                @pl.when(used + k * zrows < n_sorted)
                def _():
                    getattr(zcopy(used + k * zrows, zrows), act)()


def _scatter_rows(ntab, basetab, gstart, gcount, x, slotw, n_sorted, tm, tmg):
    t, d = x.shape
    assert d == ROW_CHUNKS * LANES and tmg & (tmg - 1) == 0 and tm & (tm - 1) == 0
    grid_spec = pltpu.PrefetchScalarGridSpec(
        num_scalar_prefetch=4,
        grid=(t // tm,),
        in_specs=[pl.BlockSpec((tm, d), lambda i, *_: (i, 0)), pl.BlockSpec((SUBLANES, tm), lambda i, *_: (0, i))],
        out_specs=pl.BlockSpec(memory_space=pl.ANY),
        scratch_shapes=[pltpu.VMEM((2 * 2 * tm * ROW_PITCH, LANES), F32),
                        pltpu.VMEM((tmg // 2 * ROW_PITCH, LANES), F32),
                        pltpu.SemaphoreType.DMA((2,)), pltpu.SemaphoreType.DMA],
    )
    return pl.pallas_call(
        functools.partial(_scatter_kernel, tm=tm, tmg=tmg),
        grid_spec=grid_spec,
        out_shape=jax.ShapeDtypeStruct((n_sorted * ROW_PITCH, LANES), F32),
        compiler_params=_cparams(("arbitrary",)),
        name="moe_scatter",
    )(ntab, basetab, gstart, gcount, x, slotw)


def _grouped_ffn_kernel(te_ref, nu_ref, x_ref, wg_ref, wu_ref, wd_ref, y_ref, xb, acc):
    i = pl.program_id(0)
    j = pl.program_id(1)
    nj = pl.num_programs(1)

    @pl.when(i < nu_ref[0])
    def _():
        tmg = xb.shape[0]

        @pl.when(j == 0)
        def _():
            for c in range(ROW_CHUNKS):
                xb[:, c * LANES:(c + 1) * LANES] = x_ref[pl.ds(c, tmg, stride=ROW_PITCH), :].astype(BF16)
            acc[...] = jnp.zeros_like(acc)

        x = xb[...]
        g = jnp.dot(x, wg_ref[...], preferred_element_type=F32)
        u = jnp.dot(x, wu_ref[...], preferred_element_type=F32)
        acc[...] += jnp.dot(_swiglu_act(g, u).astype(BF16), wd_ref[...], preferred_element_type=F32)

        @pl.when(j == nj - 1)
        def _():
            for c in range(ROW_CHUNKS):
                y_ref[pl.ds(c, tmg, stride=ROW_PITCH), :] = acc[:, c * LANES:(c + 1) * LANES]
            for c in range(ROW_CHUNKS, ROW_PITCH):
                y_ref[pl.ds(c, tmg, stride=ROW_PITCH), :] = jnp.zeros((tmg, LANES), F32)

    @pl.when(jnp.logical_and(i >= nu_ref[0], j == nj - 1))
    def _():
        y_ref[...] = jnp.zeros_like(y_ref)


def _grouped_ffn(te, nused, xs, wg, wu, wd, tmg, fc, nt):
    d, f = wg.shape[1], wg.shape[2]
    nj = f // fc
    blk = tmg * ROW_PITCH

    def tile(i, nu):
        return jnp.minimum(i, nu[0] - 1)

    def chunk(i, j, nu):
        return jnp.where(i < nu[0], j, nj - 1)

    grid_spec = pltpu.PrefetchScalarGridSpec(
        num_scalar_prefetch=2,
        grid=(nt, nj),
        in_specs=[
            pl.BlockSpec((blk, LANES), lambda i, j, te, nu: (tile(i, nu), 0)),
            pl.BlockSpec((None, d, fc), lambda i, j, te, nu: (te[tile(i, nu)], 0, chunk(i, j, nu))),
            pl.BlockSpec((None, d, fc), lambda i, j, te, nu: (te[tile(i, nu)], 0, chunk(i, j, nu))),
            pl.BlockSpec((None, fc, d), lambda i, j, te, nu: (te[tile(i, nu)], chunk(i, j, nu), 0)),
        ],
        out_specs=pl.BlockSpec((blk, LANES), lambda i, j, te, nu: (i, 0)),
        scratch_shapes=[pltpu.VMEM((tmg, d), BF16), pltpu.VMEM((tmg, d), F32)],
    )
    return pl.pallas_call(
        _grouped_ffn_kernel,
        grid_spec=grid_spec,
        out_shape=jax.ShapeDtypeStruct(xs.shape, F32),
        compiler_params=_cparams(("arbitrary", "arbitrary")),
        name="moe_ffn",
    )(te, nused, xs, wg, wu, wd)


def _combine_kernel(ntab, basetab, gstart, ys_hbm, slot_ref, h_ref, gain_ref, o_ref, ybuf, yg, sem, *, tm):
    i = pl.program_id(0)
    half = lax.rem(i, 2)

    def runs(tile, which, act):
        def go(loc, src, size):
            cp = pltpu.make_async_copy(_rows(ys_hbm, src, size), _rows(ybuf, which * (2 * tm) + loc, size),
                                       sem.at[which])
            getattr(cp, act)()
        _tile_runs(tile, ntab, basetab, gstart, tm, go)

    @pl.when(i == 0)
    def _():
        runs(i, half, "start")

    @pl.when(i + 1 < pl.num_programs(0))
    def _():
        runs(i + 1, 1 - half, "start")

    runs(i, half, "wait")
    sub0 = half * (2 * tm * ROW_PITCH)
    for c in range(ROW_CHUNKS):
        yg[:, c * LANES:(c + 1) * LANES] = ybuf[pl.ds(sub0 + c, 2 * tm, stride=ROW_PITCH), :].astype(BF16)

    sw = jnp.concatenate([slot_ref[...], jnp.zeros((LANES - SUBLANES, tm), F32)], axis=0).T
    lane = lax.broadcasted_iota(jnp.int32, (tm, 2 * tm), 1).astype(F32)
    pick = jnp.where(lane == sw[:, 0:1], sw[:, 2:3], jnp.where(lane == sw[:, 1:2], sw[:, 3:4], 0.0))
    y = h_ref[...] + jnp.dot(pick.astype(BF16), yg[...], preferred_element_type=F32)
    o_ref[...] = _rms(y, gain_ref[...])


def _combine(ntab, basetab, gstart, ys, slotw, h, gain, tm):
    t, d = h.shape
    grid_spec = pltpu.PrefetchScalarGridSpec(
        num_scalar_prefetch=3,
        grid=(t // tm,),
        in_specs=[pl.BlockSpec(memory_space=pl.ANY),
                  pl.BlockSpec((SUBLANES, tm), lambda i, *_: (0, i)),
                  pl.BlockSpec((tm, d), lambda i, *_: (i, 0)),
                  pl.BlockSpec((1, d), lambda i, *_: (0, 0))],
        out_specs=pl.BlockSpec((tm, d), lambda i, *_: (i, 0)),
        scratch_shapes=[pltpu.VMEM((2 * 2 * tm * ROW_PITCH, LANES), F32), pltpu.VMEM((2 * tm, d), BF16),
                        pltpu.SemaphoreType.DMA((2,))],
    )
    return pl.pallas_call(
        functools.partial(_combine_kernel, tm=tm),
        grid_spec=grid_spec,
        out_shape=jax.ShapeDtypeStruct((t, d), F32),
        compiler_params=_cparams(("arbitrary",)),
        name="moe_combine",
    )(ntab, basetab, gstart, ys, slotw, h, gain)


def _half_if_aligned(n):
    return n // 2 if n % (2 * LANES) == 0 else n


def _tiles(seq, d_ff, d_ff_expert):
    return dict(
        tm_in=1024,
        tq=min(1024, seq),
        tm_mg=512,
        fc_ffn=_half_if_aligned(d_ff),
        tm_sc=512,
        tmg=512, fc_moe=_half_if_aligned(d_ff_expert),
    )


def kernel(x, meta_tokens, norm_mix, w_in, attn_sinks, w_attn_br, w_pool_grp, pool_scale, w_out, norm_ffn,
           dense_w_gate, dense_w_up, dense_w_down, moe_router, moe_w_gate, moe_w_up, moe_w_down, norm_final):
    batch, seq, d = x.shape
    depth = w_in.shape[0]
    assert depth == 2 and dense_w_gate.shape[0] == 1 and moe_router.shape[0] == 1, "dense layer then expert layer"
    assert moe_router.shape[2] == N_EXPERTS and w_in.shape[2] == N_HEADS * HEAD_DIM + 2 * KV_WIDTH + POOL_WIDTH + 2 * d
    t = batch * seq
    cfg = _tiles(seq, dense_w_gate.shape[2], moe_w_gate.shape[3])
    bf = lambda a: a.astype(BF16)

    bias_np, mrow_np, sinkpos_np = _attn_bias_tables()
    mrow = jnp.asarray(mrow_np)
    pool_inv_main = jnp.asarray(_pool_inverse_counts(cfg["tm_mg"], False))
    pool_inv_meta = jnp.asarray(_pool_inverse_counts(META_ROWS, True))

    h = x.reshape(t, d)
    hm = jnp.concatenate([meta_tokens.astype(F32), jnp.zeros((META_ROWS - N_META, d), F32)], axis=0)

    out = None
    for layer in range(depth):
        gain = norm_mix[layer].reshape(1, d)
        w_in_bf = bf(w_in[layer])
        sink_row = jnp.repeat(attn_sinks[layer].astype(F32) * LOG2E, KCOLS).reshape(N_HEADS // 2, 1, 2 * KCOLS)
        bias = jnp.asarray(bias_np) + (jnp.asarray(sinkpos_np) * sink_row)[None]
        wa, wo = bf(w_attn_br[layer]), bf(w_out[layer])
        wp = bf(w_pool_grp[layer])
        ps = pool_scale[layer].reshape(1, d)
        gain_ffn = norm_ffn[layer].reshape(1, d)

        qm, kkm_all, vvm_all, um, gm = _inproj(hm, gain, w_in_bf, META_ROWS)
        kkm, vvm = kkm_all[:N_META], vvm_all[:N_META]
        um16 = um[:N_META]
        q, kk, vv, u, g = _inproj(h, gain, w_in_bf, cfg["tm_in"])
        attn = _attention(q, kk, vv, kkm, vvm, bias, mrow, batch=batch, seq=seq, tq=cfg["tq"], meta_mode=False)

        u_before = um16
        if layer == 0:
            attn_m = _attention(qm, kkm_all, vvm_all, kkm, vvm, bias, mrow,
                                batch=1, seq=META_ROWS, tq=META_ROWS, meta_mode=True)
            ffn = (bf(dense_w_gate[0]), bf(dense_w_up[0]), bf(dense_w_down[0]))
            hm, = _merge(attn_m, um, jnp.zeros_like(u_before), gm, hm, wa, wp, ps, wo, pool_inv_meta, gain_ffn,
                         ffn=ffn, fc=cfg["fc_ffn"], batch=1, seq=META_ROWS, tm=META_ROWS)
            h, = _merge(attn, u, u_before, g, h, wa, wp, ps, wo, pool_inv_main, gain_ffn,
                        ffn=ffn, fc=cfg["fc_ffn"], batch=batch, seq=seq, tm=cfg["tm_mg"])
        else:
            r = moe_router[0].astype(F32)
            r_hi = r.astype(BF16)
            r_lo = (r - r_hi.astype(F32)).astype(BF16)
            rt = jnp.concatenate([r_hi.T, r_lo.T], axis=0)
            h, hn, ri, rw = _merge(attn, u, u_before, g, h, wa, wp, ps, wo, pool_inv_main, gain_ffn, rt=rt,
                                   batch=batch, seq=seq, tm=cfg["tm_mg"])
            wg, wu, wd = bf(moe_w_gate[0]), bf(moe_w_up[0]), bf(moe_w_down[0])
            out = _moe(h, hn, ri, rw, wg, wu, wd, norm_final.reshape(1, d), cfg)
    return out.reshape(batch, seq, d)


def _moe(h, hn, ri, rw, wg, wu, wd, gain_final, cfg):
    t, d = h.shape
    tmg, tm = cfg["tmg"], cfg["tm_sc"]
    nt = 2 * t // tmg + N_EXPERTS
    nt_pad = -(-nt // LANES) * LANES
    slotw, tab, fin = _route_tables(ri, rw, tm, tmg, nt_pad)
    ntab = tab[:, 0, :E_ROWS].reshape(-1)
    basetab = tab[:, 1, :E_ROWS].reshape(-1)
    te, nused, gstart, gcount = fin[0, :nt], fin[1, :1], fin[2, :E_ROWS], fin[3, :E_ROWS]
    xs = _scatter_rows(ntab, basetab, gstart, gcount, hn, slotw, nt * tmg, tm, tmg)
    ys = _grouped_ffn(te, nused, xs, wg, wu, wd, tmg, cfg["fc_moe"], nt)
    return _combine(ntab, basetab, gstart, ys, slotw, h, gain_final, tm)
```

```python
import functools

import numpy as np
import jax
import jax.numpy as jnp
from jax import lax
from jax.experimental import pallas as pl
from jax.experimental.pallas import tpu as pltpu

F32 = jnp.float32
BF16 = jnp.bfloat16

N_HEADS = 16
HEAD_DIM = 64
N_KV_HEADS = 2
KV_WIDTH = N_KV_HEADS * HEAD_DIM
WINDOW = 128
N_META = 16
POOL_WINDOWS = (2, 4, 8, 16)
N_POOL_GROUPS = 4
POOL_GROUP_DIM = 128
POOL_WIDTH = N_POOL_GROUPS * POOL_GROUP_DIM
N_EXPERTS = 8
RMS_EPS = 1e-5
NEG_BIAS = -1e30
LOG2E = 1.4426950408889634

LANES = 128
SUBLANES = 8
META_ROWS = 128
VMEM_LIMIT = 56 * 1024 * 1024
GATE_COLS = 512

QB = 64
KB = WINDOW + QB
KCOLS = 256
SINK_COL = KB + N_META


def _cparams(sem, vmem=VMEM_LIMIT):
    return pltpu.CompilerParams(dimension_semantics=sem, vmem_limit_bytes=vmem)


def _resident(shape):
    zeros = (0,) * len(shape)
    return pl.BlockSpec(shape, lambda *_: zeros, pipeline_mode=pl.Buffered(1))


def _rms(x, gain):
    ms = jnp.mean(x * x, axis=-1, keepdims=True)
    return x * lax.rsqrt(ms + RMS_EPS) * gain


def _inproj_kernel(h_ref, gain_ref, w_ref, q_ref, kk_ref, vv_ref, u_ref, g_ref, *, d_model):
    xn = _rms(h_ref[...], gain_ref[...]).astype(BF16)
    aw = N_HEADS * HEAD_DIM
    qscale = (HEAD_DIM ** -0.5) * LOG2E
    q = jnp.dot(xn, w_ref[:, 0:aw], preferred_element_type=F32)
    q_ref[...] = (q * qscale).astype(BF16)
    kv = jnp.dot(xn, w_ref[:, aw:aw + 2 * KV_WIDTH], preferred_element_type=F32)
    k = kv[:, :KV_WIDTH]
    v = kv[:, KV_WIDTH:]
    kk_ref[:, :KV_WIDTH] = k.astype(BF16)
    kk_ref[:, KV_WIDTH:] = pltpu.roll(k, HEAD_DIM, 1).astype(BF16)
    vv_ref[:, :KV_WIDTH] = v.astype(BF16)
    vv_ref[:, KV_WIDTH:] = pltpu.roll(v, HEAD_DIM, 1).astype(BF16)
    o = aw + 2 * KV_WIDTH
    u_ref[...] = jnp.dot(xn, w_ref[:, o:o + POOL_WIDTH], preferred_element_type=F32).astype(BF16)
    o += POOL_WIDTH
    for c in range(2 * d_model // GATE_COLS):
        cols = slice(c * GATE_COLS, (c + 1) * GATE_COLS)
        g = jnp.dot(xn, w_ref[:, o + cols.start:o + cols.stop], preferred_element_type=F32)
        g_ref[:, cols] = jax.nn.sigmoid(g).astype(BF16)


def _inproj(h, gain, w_bf, tm):
    t, d = h.shape
    n = w_bf.shape[1]
    aw = N_HEADS * HEAD_DIM
    row = lambda w: pl.BlockSpec((tm, w), lambda i: (i, 0))
    return pl.pallas_call(
        functools.partial(_inproj_kernel, d_model=d),
        grid=(t // tm,),
        in_specs=[row(d), _resident((1, d)), _resident((d, n))],
        out_specs=[row(aw), row(2 * KV_WIDTH), row(2 * KV_WIDTH), row(POOL_WIDTH), row(2 * d)],
        out_shape=[
            jax.ShapeDtypeStruct((t, aw), BF16),
            jax.ShapeDtypeStruct((t, 2 * KV_WIDTH), BF16),
            jax.ShapeDtypeStruct((t, 2 * KV_WIDTH), BF16),
            jax.ShapeDtypeStruct((t, POOL_WIDTH), BF16),
            jax.ShapeDtypeStruct((t, 2 * d), BF16),
        ],
        compiler_params=_cparams(("parallel",)),
        name="inproj",
    )(h, gain, w_bf)


def _attn_bias_tables():
    slopes = np.array([2.0 ** (-8.0 * (h + 1) / N_HEADS) for h in range(N_HEADS)], dtype=np.float64)
    i = np.arange(QB)[:, None]
    c = np.arange(KCOLS)[None, :]
    d_band = WINDOW + i - c
    ok_band = (c < KB) & (d_band >= 0) & (d_band < WINDOW)
    m = c - KB
    is_meta = (c >= KB) & (c < KB + N_META)
    d_meta = N_META + i - m
    tbl = np.full((4, N_HEADS, QB, KCOLS), NEG_BIAS, dtype=np.float64)
    for var in range(4):
        if var == 0:
            okb = ok_band
        elif var == 1:
            okb = ok_band & (c >= WINDOW)
        elif var == 2:
            okb = ok_band & (c >= WINDOW - QB)
        else:
            okb = np.zeros_like(ok_band)
        okm = np.broadcast_to(is_meta, (QB, KCOLS)) & ((d_meta - N_META >= 0) if var == 3 else True)
        okb = np.broadcast_to(okb, (QB, KCOLS))
        for h in range(N_HEADS):
            t = tbl[var, h]
            t[okb] = (-slopes[h] * LOG2E * d_band)[okb]
            t[okm] = (-slopes[h] * LOG2E * d_meta)[okm]
            t[:, SINK_COL] = 0.0
    tbl = tbl.reshape(4, N_HEADS // 2, 2, QB, KCOLS).transpose(0, 1, 3, 2, 4)
    tbl = tbl.reshape(4, N_HEADS // 2, QB, 2 * KCOLS)
    mrow = np.zeros((N_HEADS, KCOLS), dtype=np.float64)
    mrow[:, KB:KB + N_META] = (-slopes * LOG2E)[:, None]
    mrow = mrow.reshape(N_HEADS // 2, 1, 2 * KCOLS)
    sink_pos = np.zeros((N_HEADS // 2, 1, 2 * KCOLS), dtype=np.float32)
    sink_pos[:, 0, SINK_COL] = 1.0
    sink_pos[:, 0, KCOLS + SINK_COL] = 1.0
    return tbl.astype(np.float32), mrow.astype(np.float32), sink_pos


def _attn_kernel(q_ref, kk_ref, vv_ref, kkh_ref, vvh_ref, kkm_ref, vvm_ref, bias_ref, mrow_ref,
                 o_ref, khead, vhead, bmat, vmat, *, tq, meta_mode):
    n_hb = bmat.shape[0]
    j = pl.program_id(1)
    hd = HEAD_DIM

    khead[0:WINDOW, :] = kkh_ref[...]
    khead[WINDOW:, :] = kk_ref[0:WINDOW, :]
    vhead[0:WINDOW, :] = vvh_ref[...]
    vhead[WINDOW:, :] = vv_ref[0:WINDOW, :]

    def place(dst, par, src, rows, kvh, row0, nrows):
        lo_col = 0 if kvh == 0 else 2 * hd
        hi_col = 3 * hd if kvh == 0 else hd
        dst[par, kvh, row0:row0 + nrows, 0:hd] = src[rows, lo_col:lo_col + hd]
        dst[par, kvh, KCOLS + row0:KCOLS + row0 + nrows, hd:2 * hd] = src[rows, hi_col:hi_col + hd]

    @pl.when(jnp.logical_and(pl.program_id(0) == 0, j == 0))
    def _():
        bmat[...] = jnp.zeros_like(bmat)
        for par in range(n_hb):
            for kvh in range(N_KV_HEADS):
                vmat[par, kvh, :, 0:LANES] = jnp.zeros((2 * KCOLS, LANES), BF16)
                vmat[par, kvh, 0:KCOLS, LANES:LANES + hd] = jnp.ones((KCOLS, hd), BF16)
                vmat[par, kvh, 0:KCOLS, LANES + hd:2 * LANES] = jnp.zeros((KCOLS, hd), BF16)
                vmat[par, kvh, KCOLS:2 * KCOLS, LANES:LANES + hd] = jnp.zeros((KCOLS, hd), BF16)
                vmat[par, kvh, KCOLS:2 * KCOLS, LANES + hd:2 * LANES] = jnp.ones((KCOLS, hd), BF16)
                place(bmat, par, kkm_ref, slice(None), kvh, KB, N_META)
                place(vmat, par, vvm_ref, slice(None), kvh, KB, N_META)

    ppk = N_HEADS // 2 // N_KV_HEADS

    for s in range(n_hb):
        o = s * QB
        if meta_mode:
            var = 3
            pos0 = float(o - N_META)
        else:
            var = jnp.where(j == 0, s + 1, 0) if s < 2 else 0
            pos0 = (j * tq + o).astype(F32)
        band = slice(o, o + KB) if s < 2 else slice(o - WINDOW, o + QB)
        ksrc, vsrc = (khead, vhead) if s < 2 else (kk_ref, vv_ref)
        for kvh in range(N_KV_HEADS):
            place(bmat, s, ksrc, band, kvh, 0, KB)
            place(vmat, s, vsrc, band, kvh, 0, KB)
        for kvh in range(N_KV_HEADS):
            pairs = range(kvh * ppk, (kvh + 1) * ppk)
            qs = jnp.concatenate([q_ref[o:o + QB, p * LANES:(p + 1) * LANES] for p in pairs], axis=0)
            sc = lax.dot_general(qs, bmat[s, kvh], (((1,), (1,)), ((), ())), preferred_element_type=F32)
            probs = []
            for n, p in enumerate(pairs):
                sp = sc[n * QB:(n + 1) * QB, :] + (bias_ref[var, p] + mrow_ref[p] * pos0)
                m0 = jnp.max(sp[:, :KCOLS], axis=1, keepdims=True)
                m1 = jnp.max(sp[:, KCOLS:], axis=1, keepdims=True)
                pr = jnp.concatenate([jnp.exp2(sp[:, :KCOLS] - m0), jnp.exp2(sp[:, KCOLS:] - m1)], axis=1)
                probs.append(pr.astype(BF16))
            ov = jnp.dot(jnp.concatenate(probs, axis=0), vmat[s, kvh], preferred_element_type=F32)
            for n, p in enumerate(pairs):
                on = ov[n * QB:(n + 1) * QB, :]
                o_ref[o:o + QB, p * LANES:(p + 1) * LANES] = (on[:, :LANES] / on[:, LANES:]).astype(BF16)


def _attention(q, kk, vv, kkm, vvm, bias, mrow, *, batch, seq, tq, meta_mode):
    aw = N_HEADS * HEAD_DIM
    nj = seq // tq
    hb = tq // WINDOW if not meta_mode else 1
    n_hb = tq // QB
    assert tq % WINDOW == 0

    def row_map(b, j):
        return (b * nj + j, 0)

    def halo_map(b, j):
        if meta_mode:
            return (0, 0)
        return (jnp.maximum(b * nj * hb + j * hb - 1, b * nj * hb), 0)

    const2 = lambda b, j: (0, 0)
    return pl.pallas_call(
        functools.partial(_attn_kernel, tq=tq, meta_mode=meta_mode),
        grid=(batch, nj),
        in_specs=[
            pl.BlockSpec((tq, aw), row_map),
            pl.BlockSpec((tq, 2 * KV_WIDTH), row_map),
            pl.BlockSpec((tq, 2 * KV_WIDTH), row_map),
            pl.BlockSpec((WINDOW, 2 * KV_WIDTH), halo_map),
            pl.BlockSpec((WINDOW, 2 * KV_WIDTH), halo_map),
            pl.BlockSpec((N_META, 2 * KV_WIDTH), const2),
            pl.BlockSpec((N_META, 2 * KV_WIDTH), const2),
            pl.BlockSpec(bias.shape, lambda b, j: (0, 0, 0, 0)),
            pl.BlockSpec(mrow.shape, lambda b, j: (0, 0, 0)),
        ],
        out_specs=pl.BlockSpec((tq, aw), row_map),
        out_shape=jax.ShapeDtypeStruct((batch * seq, aw), BF16),
        scratch_shapes=[
            pltpu.VMEM((2 * WINDOW, 2 * KV_WIDTH), BF16),
            pltpu.VMEM((2 * WINDOW, 2 * KV_WIDTH), BF16),
            pltpu.VMEM((n_hb, N_KV_HEADS, 2 * KCOLS, LANES), BF16),
            pltpu.VMEM((n_hb, N_KV_HEADS, 2 * KCOLS, 2 * LANES), BF16),
        ],
        compiler_params=_cparams(("arbitrary", "arbitrary")),
        name="attn_meta" if meta_mode else "attn",
    )(q, kk, vv, kk, vv, kkm, vvm, bias, mrow)


POOL_HALO = 16


def _pool_inverse_counts(tm, clip_at_row0):
    t = np.arange(tm, dtype=np.float64)[None, :, None]
    w = np.asarray(POOL_WINDOWS, dtype=np.float64)[:, None, None]
    cnt = np.minimum(t + 1, w) if clip_at_row0 else np.broadcast_to(w, (N_POOL_GROUPS, tm, 1))
    return (1.0 / cnt).astype(np.float32)


def _merge_kernel(*refs, with_router, fc):
    (attn_ref, u_ref, uh_ref, um_ref, g_ref, h_ref, wa_ref, wp_ref, ps_ref, wo_ref, pinv_ref,
     gain_ref) = refs[:12]
    if with_router:
        rt_ref, hout_ref, hn_ref, ri_ref, rw_ref = refs[12:]
    else:
        wg_ref, wu_ref, wd_ref, hout_ref = refs[12:]

    j = pl.program_id(1)
    d = h_ref.shape[1]
    a = jnp.dot(attn_ref[...], wa_ref[...], preferred_element_type=F32)

    before = jnp.where(j == 0, um_ref[...], uh_ref[...])
    x = jnp.concatenate([before, u_ref[...]], axis=0).astype(F32)
    parts = []
    for g, w in enumerate(POOL_WINDOWS):
        xg = x[:, g * POOL_GROUP_DIM:(g + 1) * POOL_GROUP_DIM]
        s, k = xg, 1
        while k < w:
            s = s + pltpu.roll(s, k, 0)
            k *= 2
        pooled = s[POOL_HALO:] * pinv_ref[g] - xg[POOL_HALO:]
        parts.append(jnp.dot(pooled.astype(BF16), wp_ref[g], preferred_element_type=F32))
    pb = jnp.concatenate(parts, axis=1) * ps_ref[...]
    gates = g_ref[...]
    merged = gates[:, :d].astype(F32) * a + gates[:, d:].astype(F32) * pb
    out = jnp.dot(merged.astype(BF16), wo_ref[...], preferred_element_type=F32)
    hnew = h_ref[...] + out
    xn = _rms(hnew, gain_ref[...])
    xh = xn.astype(BF16)
    if not with_router:
        hout_ref[...] = _dense_swiglu(xh, hnew, wg_ref, wu_ref, wd_ref, fc)
    else:
        hout_ref[...] = hnew
        hn_ref[...] = xh
        xl = (xn - xh.astype(F32)).astype(BF16)
        nt = (((1,), (1,)), ((), ()))
        t_hi = lax.dot_general(rt_ref[...], xh, nt, preferred_element_type=F32)
        t_lo = lax.dot_general(rt_ref[...], xl, nt, preferred_element_type=F32)
        lg = t_hi[:N_EXPERTS] + t_hi[N_EXPERTS:] + t_lo[:N_EXPERTS]
        eid = lax.broadcasted_iota(jnp.int32, lg.shape, 0).astype(F32)
        m1 = jnp.max(lg, axis=0, keepdims=True)
        i1 = jnp.min(jnp.where(lg == m1, eid, float(N_EXPERTS)), axis=0, keepdims=True)
        lg2 = jnp.where(eid == i1, -jnp.inf, lg)
        m2 = jnp.max(lg2, axis=0, keepdims=True)
        i2 = jnp.min(jnp.where(lg2 == m2, eid, float(N_EXPERTS)), axis=0, keepdims=True)
        e2 = jnp.exp(m2 - m1)
        w1 = 1.0 / (1.0 + e2)
        w2 = e2 / (1.0 + e2)
        row = lax.broadcasted_iota(jnp.int32, lg.shape, 0)
        ri_ref[...] = jnp.where(row == 0, i1, jnp.where(row == 1, i2, 0.0)).astype(jnp.int32)
        rw_ref[...] = jnp.where(row == 0, w1, jnp.where(row == 1, w2, 0.0))


def _merge(attn, u, u_meta, gates, h, wa, wp, pscale, wo, pool_inv, gain, *, rt=None, ffn=None, fc=None,
           batch, seq, tm):
    t, d = h.shape
    nj = seq // tm
    hs = tm // POOL_HALO
    with_router = rt is not None
    assert with_router != (ffn is not None)

    def row_map(b, j):
        return (b * nj + j, 0)

    def halo_map(b, j):
        return (jnp.maximum((b * nj + j) * hs - 1, 0), 0)

    in_specs = [
        pl.BlockSpec((tm, attn.shape[1]), row_map),
        pl.BlockSpec((tm, POOL_WIDTH), row_map),
        pl.BlockSpec((POOL_HALO, POOL_WIDTH), halo_map),
        _resident((POOL_HALO, POOL_WIDTH)),
        pl.BlockSpec((tm, 2 * d), row_map),
        pl.BlockSpec((tm, d), row_map),
        _resident(wa.shape),
        _resident(wp.shape),
        _resident((1, d)),
        _resident(wo.shape),
        _resident(pool_inv.shape),
        _resident((1, d)),
    ]
    args = [attn, u, u, u_meta, gates, h, wa, wp, pscale, wo, pool_inv, gain]
    out_specs = [pl.BlockSpec((tm, d), row_map)]
    out_shape = [jax.ShapeDtypeStruct((t, d), F32)]
    if with_router:
        in_specs.append(_resident(rt.shape))
        args.append(rt)
        lane_map = lambda b, j: (0, b * nj + j)
        out_specs += [pl.BlockSpec((tm, d), row_map), pl.BlockSpec((SUBLANES, tm), lane_map),
                      pl.BlockSpec((SUBLANES, tm), lane_map)]
        out_shape += [jax.ShapeDtypeStruct((t, d), BF16), jax.ShapeDtypeStruct((SUBLANES, t), jnp.int32),
                      jax.ShapeDtypeStruct((SUBLANES, t), F32)]
    else:
        in_specs += [_resident(w.shape) for w in ffn]
        args += list(ffn)
    return pl.pallas_call(
        functools.partial(_merge_kernel, with_router=with_router, fc=fc),
        grid=(batch, nj),
        in_specs=in_specs,
        out_specs=out_specs,
        out_shape=out_shape,
        compiler_params=_cparams(("parallel", "parallel")),
        name="merge_router" if with_router else "merge_ffn",
    )(*args)


def _swiglu_act(g, u):
    return (g * jax.nn.sigmoid(g)) * u


def _dense_swiglu(x, residual, wg_ref, wu_ref, wd_ref, fc):
    acc = residual
    for c in range(wg_ref.shape[1] // fc):
        cols = slice(c * fc, (c + 1) * fc)
        g = jnp.dot(x, wg_ref[:, cols], preferred_element_type=F32)
        u = jnp.dot(x, wu_ref[:, cols], preferred_element_type=F32)
        acc = acc + jnp.dot(_swiglu_act(g, u).astype(BF16), wd_ref[cols, :], preferred_element_type=F32)
    return acc


E_ROWS = 16


def _col_to_lanes(col, width):
    r = lax.broadcasted_iota(jnp.int32, (E_ROWS, width), 0)
    c = lax.broadcasted_iota(jnp.int32, (E_ROWS, width), 1)
    return jnp.sum(jnp.where(r == c, col, 0.0), axis=0, keepdims=True)


def _excl_cumsum_col(col):
    r = lax.broadcasted_iota(jnp.int32, (E_ROWS, E_ROWS), 0)
    c = lax.broadcasted_iota(jnp.int32, (E_ROWS, E_ROWS), 1)
    return jnp.sum(jnp.where(c < r, _col_to_lanes(col, E_ROWS), 0.0), axis=1, keepdims=True)


def _route_tab_kernel(ri_ref, rw_ref, tri_ref, slot_ref, tab_ref, fin_ref, carry, *, tn, tmg, nt_pad):
    i = pl.program_id(0)

    @pl.when(i == 0)
    def _():
        carry[...] = jnp.zeros_like(carry)

    eid = lax.broadcasted_iota(jnp.int32, (E_ROWS, tn), 0)
    ri = ri_ref[...]
    oh0 = (eid == ri[0:1, :]).astype(F32)
    oh1 = (eid == ri[1:2, :]).astype(F32)
    oh = oh0 + oh1
    cum = jnp.dot(oh.astype(BF16), tri_ref[...], preferred_element_type=F32)
    cnt = jnp.sum(oh, axis=1, keepdims=True)
    base = _excl_cumsum_col(cnt) + cum
    s0 = jnp.sum(oh0 * base, axis=0, keepdims=True)
    s1 = jnp.sum(oh1 * base, axis=0, keepdims=True)
    row = lax.broadcasted_iota(jnp.int32, (SUBLANES, tn), 0)
    rw = rw_ref[...]
    slot_ref[...] = jnp.where(row == 0, s0, jnp.where(row == 1, s1, jnp.where(row == 2, rw[0:1, :],
                              jnp.where(row == 3, rw[1:2, :], 0.0))))
    trow = lax.broadcasted_iota(jnp.int32, (SUBLANES, LANES), 0)
    tab_ref[...] = jnp.where(trow == 0, _col_to_lanes(cnt, LANES),
                             jnp.where(trow == 1, _col_to_lanes(carry[...], LANES), 0.0)).astype(jnp.int32)
    carry[...] += cnt

    @pl.when(i == pl.num_programs(0) - 1)
    def _():
        total = carry[...]
        padded = jnp.ceil(total / tmg) * tmg
        st = _excl_cumsum_col(padded)
        ends = st + padded
        tile0 = (lax.broadcasted_iota(jnp.int32, (E_ROWS, nt_pad), 1) * tmg).astype(F32)
        texp = jnp.minimum(jnp.sum((tile0 >= ends).astype(F32), axis=0, keepdims=True), N_EXPERTS - 1.0)
        nused = jnp.max(ends, axis=0, keepdims=True) / tmg
        frow = lax.broadcasted_iota(jnp.int32, (SUBLANES, nt_pad), 0)
        fin_ref[...] = jnp.where(frow == 0, texp, jnp.where(frow == 1, nused, jnp.where(
            frow == 2, _col_to_lanes(st, nt_pad), jnp.where(frow == 3, _col_to_lanes(total, nt_pad), 0.0)))
        ).astype(jnp.int32)


def _route_tables(ri, rw, tn, tmg, nt_pad):
    t = ri.shape[1]
    nb = t // tn
    tri = jnp.asarray(np.triu(np.ones((tn, tn), dtype=np.float32), k=1), dtype=BF16)
    lane_blk = pl.BlockSpec((SUBLANES, tn), lambda i: (0, i))
    return pl.pallas_call(
        functools.partial(_route_tab_kernel, tn=tn, tmg=tmg, nt_pad=nt_pad),
        grid=(nb,),
        in_specs=[lane_blk, lane_blk, pl.BlockSpec((tn, tn), lambda i: (0, 0))],
        out_specs=[lane_blk, pl.BlockSpec((None, SUBLANES, LANES), lambda i: (i, 0, 0)),
                   pl.BlockSpec((SUBLANES, nt_pad), lambda i: (0, 0))],
        out_shape=[jax.ShapeDtypeStruct((SUBLANES, t), F32), jax.ShapeDtypeStruct((nb, SUBLANES, LANES), jnp.int32),
                   jax.ShapeDtypeStruct((SUBLANES, nt_pad), jnp.int32)],
        scratch_shapes=[pltpu.VMEM((E_ROWS, 1), F32)],
        compiler_params=_cparams(("arbitrary",)),
        name="route_tab",
    )(ri, rw, tri)


ROW_CHUNKS = 8
ROW_PITCH = 9


def _pieces(n, max_log2, fn):
    for b in reversed(range(max_log2 + 1)):
        above = (n >> (b + 1)) << (b + 1)

        @pl.when(((n >> b) & 1) == 1)
        def _():
            fn(above, 1 << b)


def _tile_runs(i, ntab, basetab, gstart, tm, fn):
    off = jnp.int32(0)
    for e in range(N_EXPERTS):
        n_e = ntab[i * E_ROWS + e]
        dst = gstart[e] + basetab[i * E_ROWS + e]
        _pieces(n_e, tm.bit_length() - 1, lambda o, size, off=off, dst=dst: fn(off + o, dst + o, size))
        off = off + n_e


def _rows(ref, start, size):
    return ref.at[pl.ds(start * ROW_PITCH, size * ROW_PITCH)]


def _scatter_kernel(ntab, basetab, gstart, gcount, x_ref, slot_ref, xs_hbm, sbuf, zbuf, sem, zsem, *, tm, tmg):
    i = pl.program_id(0)
    last = pl.num_programs(0) - 1
    n_sorted = xs_hbm.shape[0] // ROW_PITCH
    half = lax.rem(i, 2)
    slots = slot_ref[...]
    rowid = lax.broadcasted_iota(jnp.int32, (2 * tm, tm), 0).astype(F32)
    perm = jnp.where(jnp.logical_or(rowid == slots[0:1, :], rowid == slots[1:2, :]), 1.0, 0.0).astype(BF16)
    srt = jnp.dot(perm, x_ref[...], preferred_element_type=F32)
    @pl.when(i == 0)
    def _():
        sbuf[...] = jnp.zeros_like(sbuf)

    sub0 = half * (2 * tm * ROW_PITCH)
    for c in range(ROW_CHUNKS):
        sbuf[pl.ds(sub0 + c, 2 * tm, stride=ROW_PITCH), :] = srt[:, c * LANES:(c + 1) * LANES]

    def runs(tile, which, act):
        def go(src, dst, size):
            cp = pltpu.make_async_copy(_rows(sbuf, which * (2 * tm) + src, size), _rows(xs_hbm, dst, size),
                                       sem.at[which])
            getattr(cp, act)()
        _tile_runs(tile, ntab, basetab, gstart, tm, go)

    @pl.when(i > 0)
    def _():
        runs(i - 1, 1 - half, "wait")

    runs(i, half, "start")

    @pl.when(i == last)
    def _():
        runs(i, half, "wait")
        zbuf[...] = jnp.zeros_like(zbuf)
        zrows = zbuf.shape[0] // ROW_PITCH

        def zcopy(dst, size):
            return pltpu.make_async_copy(_rows(zbuf, 0, size), _rows(xs_hbm, dst, size), zsem)

        used = gstart[N_EXPERTS]
        for act in ("start", "wait"):
            for e in range(N_EXPERTS):
                end = gstart[e] + gcount[e]
                npad = (tmg - (gcount[e] & (tmg - 1))) & (tmg - 1)
                _pieces(npad, zrows.bit_length() - 1,
                        lambda o, size, end=end, act=act: getattr(zcopy(end + o, size), act)())
            for k in range(N_EXPERTS * tmg // zrows):
                @pl.when(used + k * zrows < n_sorted)
                def _():
                    getattr(zcopy(used + k * zrows, zrows), act)()


def _scatter_rows(ntab, basetab, gstart, gcount, x, slotw, n_sorted, tm, tmg):
    t, d = x.shape
    assert d == ROW_CHUNKS * LANES and tmg & (tmg - 1) == 0 and tm & (tm - 1) == 0
    grid_spec = pltpu.PrefetchScalarGridSpec(
        num_scalar_prefetch=4,
        grid=(t // tm,),
        in_specs=[pl.BlockSpec((tm, d), lambda i, *_: (i, 0)), pl.BlockSpec((SUBLANES, tm), lambda i, *_: (0, i))],
        out_specs=pl.BlockSpec(memory_space=pl.ANY),
        scratch_shapes=[pltpu.VMEM((2 * 2 * tm * ROW_PITCH, LANES), F32),
                        pltpu.VMEM((tmg // 2 * ROW_PITCH, LANES), F32),
                        pltpu.SemaphoreType.DMA((2,)), pltpu.SemaphoreType.DMA],
    )
    return pl.pallas_call(
        functools.partial(_scatter_kernel, tm=tm, tmg=tmg),
        grid_spec=grid_spec,
        out_shape=jax.ShapeDtypeStruct((n_sorted * ROW_PITCH, LANES), F32),
        compiler_params=_cparams(("arbitrary",)),
        name="moe_scatter",
    )(ntab, basetab, gstart, gcount, x, slotw)


def _grouped_ffn_kernel(te_ref, nu_ref, x_ref, wg_ref, wu_ref, wd_ref, y_ref, xb, acc):
    i = pl.program_id(0)
    j = pl.program_id(1)
    nj = pl.num_programs(1)

    @pl.when(i < nu_ref[0])
    def _():
        tmg = xb.shape[0]

        @pl.when(j == 0)
        def _():
            for c in range(ROW_CHUNKS):
                xb[:, c * LANES:(c + 1) * LANES] = x_ref[pl.ds(c, tmg, stride=ROW_PITCH), :].astype(BF16)
            acc[...] = jnp.zeros_like(acc)

        x = xb[...]
        g = jnp.dot(x, wg_ref[...], preferred_element_type=F32)
        u = jnp.dot(x, wu_ref[...], preferred_element_type=F32)
        acc[...] += jnp.dot(_swiglu_act(g, u).astype(BF16), wd_ref[...], preferred_element_type=F32)

        @pl.when(j == nj - 1)
        def _():
            for c in range(ROW_CHUNKS):
                y_ref[pl.ds(c, tmg, stride=ROW_PITCH), :] = acc[:, c * LANES:(c + 1) * LANES]
            for c in range(ROW_CHUNKS, ROW_PITCH):
                y_ref[pl.ds(c, tmg, stride=ROW_PITCH), :] = jnp.zeros((tmg, LANES), F32)

    @pl.when(jnp.logical_and(i >= nu_ref[0], j == nj - 1))
    def _():
        y_ref[...] = jnp.zeros_like(y_ref)


def _grouped_ffn(te, nused, xs, wg, wu, wd, tmg, fc, nt):
    d, f = wg.shape[1], wg.shape[2]
    nj = f // fc
    blk = tmg * ROW_PITCH

    def tile(i, nu):
        return jnp.minimum(i, nu[0] - 1)

    def chunk(i, j, nu):
        return jnp.where(i < nu[0], j, nj - 1)

    grid_spec = pltpu.PrefetchScalarGridSpec(
        num_scalar_prefetch=2,
        grid=(nt, nj),
        in_specs=[
            pl.BlockSpec((blk, LANES), lambda i, j, te, nu: (tile(i, nu), 0)),
            pl.BlockSpec((None, d, fc), lambda i, j, te, nu: (te[tile(i, nu)], 0, chunk(i, j, nu))),
            pl.BlockSpec((None, d, fc), lambda i, j, te, nu: (te[tile(i, nu)], 0, chunk(i, j, nu))),
            pl.BlockSpec((None, fc, d), lambda i, j, te, nu: (te[tile(i, nu)], chunk(i, j, nu), 0)),
        ],
        out_specs=pl.BlockSpec((blk, LANES), lambda i, j, te, nu: (i, 0)),
        scratch_shapes=[pltpu.VMEM((tmg, d), BF16), pltpu.VMEM((tmg, d), F32)],
    )
    return pl.pallas_call(
        _grouped_ffn_kernel,
        grid_spec=grid_spec,
        out_shape=jax.ShapeDtypeStruct(xs.shape, F32),
        compiler_params=_cparams(("arbitrary", "arbitrary")),
        name="moe_ffn",
    )(te, nused, xs, wg, wu, wd)


def _combine_kernel(ntab, basetab, gstart, ys_hbm, slot_ref, h_ref, gain_ref, o_ref, ybuf, yg, sem, *, tm):
    i = pl.program_id(0)
    half = lax.rem(i, 2)

    def runs(tile, which, act):
        def go(loc, src, size):
            cp = pltpu.make_async_copy(_rows(ys_hbm, src, size), _rows(ybuf, which * (2 * tm) + loc, size),
                                       sem.at[which])
            getattr(cp, act)()
        _tile_runs(tile, ntab, basetab, gstart, tm, go)

    @pl.when(i == 0)
    def _():
        runs(i, half, "start")

    @pl.when(i + 1 < pl.num_programs(0))
    def _():
        runs(i + 1, 1 - half, "start")

    runs(i, half, "wait")
    sub0 = half * (2 * tm * ROW_PITCH)
    for c in range(ROW_CHUNKS):
        yg[:, c * LANES:(c + 1) * LANES] = ybuf[pl.ds(sub0 + c, 2 * tm, stride=ROW_PITCH), :].astype(BF16)

    sw = jnp.concatenate([slot_ref[...], jnp.zeros((LANES - SUBLANES, tm), F32)], axis=0).T
    lane = lax.broadcasted_iota(jnp.int32, (tm, 2 * tm), 1).astype(F32)
    pick = jnp.where(lane == sw[:, 0:1], sw[:, 2:3], jnp.where(lane == sw[:, 1:2], sw[:, 3:4], 0.0))
    y = h_ref[...] + jnp.dot(pick.astype(BF16), yg[...], preferred_element_type=F32)
    o_ref[...] = _rms(y, gain_ref[...])


def _combine(ntab, basetab, gstart, ys, slotw, h, gain, tm):
    t, d = h.shape
    grid_spec = pltpu.PrefetchScalarGridSpec(
        num_scalar_prefetch=3,
        grid=(t // tm,),
        in_specs=[pl.BlockSpec(memory_space=pl.ANY),
                  pl.BlockSpec((SUBLANES, tm), lambda i, *_: (0, i)),
                  pl.BlockSpec((tm, d), lambda i, *_: (i, 0)),
                  pl.BlockSpec((1, d), lambda i, *_: (0, 0))],
        out_specs=pl.BlockSpec((tm, d), lambda i, *_: (i, 0)),
        scratch_shapes=[pltpu.VMEM((2 * 2 * tm * ROW_PITCH, LANES), F32), pltpu.VMEM((2 * tm, d), BF16),
                        pltpu.SemaphoreType.DMA((2,))],
    )
    return pl.pallas_call(
        functools.partial(_combine_kernel, tm=tm),
        grid_spec=grid_spec,
        out_shape=jax.ShapeDtypeStruct((t, d), F32),
        compiler_params=_cparams(("arbitrary",)),
        name="moe_combine",
    )(ntab, basetab, gstart, ys, slotw, h, gain)


def _half_if_aligned(n):
    return n // 2 if n % (2 * LANES) == 0 else n


def _tiles(seq, d_ff, d_ff_expert):
    return dict(
        tm_in=1024,
        tq=min(1024, seq),
        tm_mg=512,
        fc_ffn=d_ff,
        tm_sc=512,
        tmg=512, fc_moe=_half_if_aligned(d_ff_expert),
    )


def kernel(x, meta_tokens, norm_mix, w_in, attn_sinks, w_attn_br, w_pool_grp, pool_scale, w_out, norm_ffn,
           dense_w_gate, dense_w_up, dense_w_down, moe_router, moe_w_gate, moe_w_up, moe_w_down, norm_final):
    batch, seq, d = x.shape
    depth = w_in.shape[0]
    assert depth == 2 and dense_w_gate.shape[0] == 1 and moe_router.shape[0] == 1, "dense layer then expert layer"
    assert moe_router.shape[2] == N_EXPERTS and w_in.shape[2] == N_HEADS * HEAD_DIM + 2 * KV_WIDTH + POOL_WIDTH + 2 * d
    t = batch * seq
    cfg = _tiles(seq, dense_w_gate.shape[2], moe_w_gate.shape[3])
    bf = lambda a: a.astype(BF16)

    bias_np, mrow_np, sinkpos_np = _attn_bias_tables()
    mrow = jnp.asarray(mrow_np)
    pool_inv_main = jnp.asarray(_pool_inverse_counts(cfg["tm_mg"], False))
    pool_inv_meta = jnp.asarray(_pool_inverse_counts(META_ROWS, True))

    h = x.reshape(t, d)
    hm = jnp.concatenate([meta_tokens.astype(F32), jnp.zeros((META_ROWS - N_META, d), F32)], axis=0)

    out = None
    for layer in range(depth):
        gain = norm_mix[layer].reshape(1, d)
        w_in_bf = bf(w_in[layer])
        sink_row = jnp.repeat(attn_sinks[layer].astype(F32) * LOG2E, KCOLS).reshape(N_HEADS // 2, 1, 2 * KCOLS)
        bias = jnp.asarray(bias_np) + (jnp.asarray(sinkpos_np) * sink_row)[None]
        wa, wo = bf(w_attn_br[layer]), bf(w_out[layer])
        wp = bf(w_pool_grp[layer])
        ps = pool_scale[layer].reshape(1, d)
        gain_ffn = norm_ffn[layer].reshape(1, d)

        qm, kkm_all, vvm_all, um, gm = _inproj(hm, gain, w_in_bf, META_ROWS)
        kkm, vvm = kkm_all[:N_META], vvm_all[:N_META]
        um16 = um[:N_META]
        q, kk, vv, u, g = _inproj(h, gain, w_in_bf, cfg["tm_in"])
        attn = _attention(q, kk, vv, kkm, vvm, bias, mrow, batch=batch, seq=seq, tq=cfg["tq"], meta_mode=False)

        u_before = um16
        if layer == 0:
            attn_m = _attention(qm, kkm_all, vvm_all, kkm, vvm, bias, mrow,
                                batch=1, seq=META_ROWS, tq=META_ROWS, meta_mode=True)
            ffn = (bf(dense_w_gate[0]), bf(dense_w_up[0]), bf(dense_w_down[0]))
            hm, = _merge(attn_m, um, jnp.zeros_like(u_before), gm, hm, wa, wp, ps, wo, pool_inv_meta, gain_ffn,
                         ffn=ffn, fc=cfg["fc_ffn"], batch=1, seq=META_ROWS, tm=META_ROWS)
            h, = _merge(attn, u, u_before, g, h, wa, wp, ps, wo, pool_inv_main, gain_ffn,
                        ffn=ffn, fc=cfg["fc_ffn"], batch=batch, seq=seq, tm=cfg["tm_mg"])
        else:
            r = moe_router[0].astype(F32)
            r_hi = r.astype(BF16)
            r_lo = (r - r_hi.astype(F32)).astype(BF16)
            rt = jnp.concatenate([r_hi.T, r_lo.T], axis=0)
            h, hn, ri, rw = _merge(attn, u, u_before, g, h, wa, wp, ps, wo, pool_inv_main, gain_ffn, rt=rt,
                                   batch=batch, seq=seq, tm=cfg["tm_mg"])
            wg, wu, wd = bf(moe_w_gate[0]), bf(moe_w_up[0]), bf(moe_w_down[0])
            out = _moe(h, hn, ri, rw, wg, wu, wd, norm_final.reshape(1, d), cfg)
    return out.reshape(batch, seq, d)


def _moe(h, hn, ri, rw, wg, wu, wd, gain_final, cfg):
    t, d = h.shape
    tmg, tm = cfg["tmg"], cfg["tm_sc"]
    nt = 2 * t // tmg + N_EXPERTS
    nt_pad = -(-nt // LANES) * LANES
    slotw, tab, fin = _route_tables(ri, rw, tm, tmg, nt_pad)
    ntab = tab[:, 0, :E_ROWS].reshape(-1)
    basetab = tab[:, 1, :E_ROWS].reshape(-1)
    te, nused, gstart, gcount = fin[0, :nt], fin[1, :1], fin[2, :E_ROWS], fin[3, :E_ROWS]
    xs = _scatter_rows(ntab, basetab, gstart, gcount, hn, slotw, nt * tmg, tm, tmg)
    ys = _grouped_ffn(te, nused, xs, wg, wu, wd, tmg, cfg["fc_moe"], nt)
    return _combine(ntab, basetab, gstart, ys, slotw, h, gain_final, tm)
```

```python
import functools

import numpy as np
import jax
import jax.numpy as jnp
from jax import lax
from jax.experimental import pallas as pl
from jax.experimental.pallas import tpu as pltpu

F32 = jnp.float32
BF16 = jnp.bfloat16

N_HEADS = 16
HEAD_DIM = 64
N_KV_HEADS = 2
KV_WIDTH = N_KV_HEADS * HEAD_DIM
WINDOW = 128
N_META = 16
POOL_WINDOWS = (2, 4, 8, 16)
N_POOL_GROUPS = 4
POOL_GROUP_DIM = 128
POOL_WIDTH = N_POOL_GROUPS * POOL_GROUP_DIM
N_EXPERTS = 8
RMS_EPS = 1e-5
NEG_BIAS = -1e30
LOG2E = 1.4426950408889634

LANES = 128
SUBLANES = 8
META_ROWS = 128
VMEM_LIMIT = 56 * 1024 * 1024
GATE_COLS = 512

QB = 64
KB = WINDOW + QB
KCOLS = 256
SINK_COL = KB + N_META


def _cparams(sem, vmem=VMEM_LIMIT):
    return pltpu.CompilerParams(dimension_semantics=sem, vmem_limit_bytes=vmem)


def _resident(shape):
    zeros = (0,) * len(shape)
    return pl.BlockSpec(shape, lambda *_: zeros, pipeline_mode=pl.Buffered(1))


def _rms(x, gain):
    ms = jnp.mean(x * x, axis=-1, keepdims=True)
    return x * lax.rsqrt(ms + RMS_EPS) * gain


def _inproj_kernel(h_ref, gain_ref, w_ref, q_ref, kk_ref, vv_ref, u_ref, g_ref, *, d_model):
    xn = _rms(h_ref[...], gain_ref[...]).astype(BF16)
    aw = N_HEADS * HEAD_DIM
    qscale = (HEAD_DIM ** -0.5) * LOG2E
    q = jnp.dot(xn, w_ref[:, 0:aw], preferred_element_type=F32)
    q_ref[...] = (q * qscale).astype(BF16)
    kv = jnp.dot(xn, w_ref[:, aw:aw + 2 * KV_WIDTH], preferred_element_type=F32)
    k = kv[:, :KV_WIDTH]
    v = kv[:, KV_WIDTH:]
    kk_ref[:, :KV_WIDTH] = k.astype(BF16)
    kk_ref[:, KV_WIDTH:] = pltpu.roll(k, HEAD_DIM, 1).astype(BF16)
    vv_ref[:, :KV_WIDTH] = v.astype(BF16)
    vv_ref[:, KV_WIDTH:] = pltpu.roll(v, HEAD_DIM, 1).astype(BF16)
    o = aw + 2 * KV_WIDTH
    u_ref[...] = jnp.dot(xn, w_ref[:, o:o + POOL_WIDTH], preferred_element_type=F32).astype(BF16)
    o += POOL_WIDTH
    for c in range(2 * d_model // GATE_COLS):
        cols = slice(c * GATE_COLS, (c + 1) * GATE_COLS)
        g = jnp.dot(xn, w_ref[:, o + cols.start:o + cols.stop], preferred_element_type=F32)
        g_ref[:, cols] = _sigmoid(g).astype(BF16)


def _inproj(h, gain, w_bf, tm):
    t, d = h.shape
    n = w_bf.shape[1]
    aw = N_HEADS * HEAD_DIM
    row = lambda w: pl.BlockSpec((tm, w), lambda i: (i, 0))
    return pl.pallas_call(
        functools.partial(_inproj_kernel, d_model=d),
        grid=(t // tm,),
        in_specs=[row(d), _resident((1, d)), _resident((d, n))],
        out_specs=[row(aw), row(2 * KV_WIDTH), row(2 * KV_WIDTH), row(POOL_WIDTH), row(2 * d)],
        out_shape=[
            jax.ShapeDtypeStruct((t, aw), BF16),
            jax.ShapeDtypeStruct((t, 2 * KV_WIDTH), BF16),
            jax.ShapeDtypeStruct((t, 2 * KV_WIDTH), BF16),
            jax.ShapeDtypeStruct((t, POOL_WIDTH), BF16),
            jax.ShapeDtypeStruct((t, 2 * d), BF16),
        ],
        compiler_params=_cparams(("parallel",)),
        name="inproj",
    )(h, gain, w_bf)


def _attn_bias_tables():
    slopes = np.array([2.0 ** (-8.0 * (h + 1) / N_HEADS) for h in range(N_HEADS)], dtype=np.float64)
    i = np.arange(QB)[:, None]
    c = np.arange(KCOLS)[None, :]
    d_band = WINDOW + i - c
    ok_band = (c < KB) & (d_band >= 0) & (d_band < WINDOW)
    m = c - KB
    is_meta = (c >= KB) & (c < KB + N_META)
    d_meta = N_META + i - m
    tbl = np.full((4, N_HEADS, QB, KCOLS), NEG_BIAS, dtype=np.float64)
    for var in range(4):
        if var == 0:
            okb = ok_band
        elif var == 1:
            okb = ok_band & (c >= WINDOW)
        elif var == 2:
            okb = ok_band & (c >= WINDOW - QB)
        else:
            okb = np.zeros_like(ok_band)
        okm = np.broadcast_to(is_meta, (QB, KCOLS)) & ((d_meta - N_META >= 0) if var == 3 else True)
        okb = np.broadcast_to(okb, (QB, KCOLS))
        for h in range(N_HEADS):
            t = tbl[var, h]
            t[okb] = (-slopes[h] * LOG2E * d_band)[okb]
            t[okm] = (-slopes[h] * LOG2E * d_meta)[okm]
            t[:, SINK_COL] = 0.0
    tbl = tbl.reshape(4, N_HEADS // 2, 2, QB, KCOLS).transpose(0, 1, 3, 2, 4)
    tbl = tbl.reshape(4, N_HEADS // 2, QB, 2 * KCOLS)
    mrow = np.zeros((N_HEADS, KCOLS), dtype=np.float64)
    mrow[:, KB:KB + N_META] = (-slopes * LOG2E)[:, None]
    mrow = mrow.reshape(N_HEADS // 2, 1, 2 * KCOLS)
    sink_pos = np.zeros((N_HEADS // 2, 1, 2 * KCOLS), dtype=np.float32)
    sink_pos[:, 0, SINK_COL] = 1.0
    sink_pos[:, 0, KCOLS + SINK_COL] = 1.0
    return tbl.astype(np.float32), mrow.astype(np.float32), sink_pos


def _attn_kernel(q_ref, kk_ref, vv_ref, kkh_ref, vvh_ref, kkm_ref, vvm_ref, bias_ref, mrow_ref,
                 o_ref, khead, vhead, bmat, vmat, *, tq, meta_mode):
    n_hb = bmat.shape[0]
    j = pl.program_id(1)
    hd = HEAD_DIM

    khead[0:WINDOW, :] = kkh_ref[...]
    khead[WINDOW:, :] = kk_ref[0:WINDOW, :]
    vhead[0:WINDOW, :] = vvh_ref[...]
    vhead[WINDOW:, :] = vv_ref[0:WINDOW, :]

    def place(dst, par, src, rows, kvh, row0, nrows):
        lo_col = 0 if kvh == 0 else 2 * hd
        hi_col = 3 * hd if kvh == 0 else hd
        dst[par, kvh, row0:row0 + nrows, 0:hd] = src[rows, lo_col:lo_col + hd]
        dst[par, kvh, KCOLS + row0:KCOLS + row0 + nrows, hd:2 * hd] = src[rows, hi_col:hi_col + hd]

    @pl.when(jnp.logical_and(pl.program_id(0) == 0, j == 0))
    def _():
        bmat[...] = jnp.zeros_like(bmat)
        for par in range(n_hb):
            for kvh in range(N_KV_HEADS):
                vmat[par, kvh, :, 0:LANES] = jnp.zeros((2 * KCOLS, LANES), BF16)
                vmat[par, kvh, 0:KCOLS, LANES:LANES + hd] = jnp.ones((KCOLS, hd), BF16)
                vmat[par, kvh, 0:KCOLS, LANES + hd:2 * LANES] = jnp.zeros((KCOLS, hd), BF16)
                vmat[par, kvh, KCOLS:2 * KCOLS, LANES:LANES + hd] = jnp.zeros((KCOLS, hd), BF16)
                vmat[par, kvh, KCOLS:2 * KCOLS, LANES + hd:2 * LANES] = jnp.ones((KCOLS, hd), BF16)
                place(bmat, par, kkm_ref, slice(None), kvh, KB, N_META)
                place(vmat, par, vvm_ref, slice(None), kvh, KB, N_META)

    ppk = N_HEADS // 2 // N_KV_HEADS

    for s in range(n_hb):
        o = s * QB
        if meta_mode:
            var = 3
            pos0 = float(o - N_META)
        else:
            var = jnp.where(j == 0, s + 1, 0) if s < 2 else 0
            pos0 = (j * tq + o).astype(F32)
        band = slice(o, o + KB) if s < 2 else slice(o - WINDOW, o + QB)
        ksrc, vsrc = (khead, vhead) if s < 2 else (kk_ref, vv_ref)
        for kvh in range(N_KV_HEADS):
            place(bmat, s, ksrc, band, kvh, 0, KB)
            place(vmat, s, vsrc, band, kvh, 0, KB)
        for kvh in range(N_KV_HEADS):
            pairs = range(kvh * ppk, (kvh + 1) * ppk)
            qs = jnp.concatenate([q_ref[o:o + QB, p * LANES:(p + 1) * LANES] for p in pairs], axis=0)
            sc = lax.dot_general(qs, bmat[s, kvh], (((1,), (1,)), ((), ())), preferred_element_type=F32)
            probs = []
            for n, p in enumerate(pairs):
                sp = sc[n * QB:(n + 1) * QB, :] + (bias_ref[var, p] + mrow_ref[p] * pos0)
                m0 = jnp.max(sp[:, :KCOLS], axis=1, keepdims=True)
                m1 = jnp.max(sp[:, KCOLS:], axis=1, keepdims=True)
                pr = jnp.concatenate([jnp.exp2(sp[:, :KCOLS] - m0), jnp.exp2(sp[:, KCOLS:] - m1)], axis=1)
                probs.append(pr.astype(BF16))
            ov = jnp.dot(jnp.concatenate(probs, axis=0), vmat[s, kvh], preferred_element_type=F32)
            for n, p in enumerate(pairs):
                on = ov[n * QB:(n + 1) * QB, :]
                o_ref[o:o + QB, p * LANES:(p + 1) * LANES] = (on[:, :LANES] / on[:, LANES:]).astype(BF16)


def _attention(q, kk, vv, kkm, vvm, bias, mrow, *, batch, seq, tq, meta_mode):
    aw = N_HEADS * HEAD_DIM
    nj = seq // tq
    hb = tq // WINDOW if not meta_mode else 1
    n_hb = tq // QB
    assert tq % WINDOW == 0

    def row_map(b, j):
        return (b * nj + j, 0)

    def halo_map(b, j):
        if meta_mode:
            return (0, 0)
        return (jnp.maximum(b * nj * hb + j * hb - 1, b * nj * hb), 0)

    const2 = lambda b, j: (0, 0)
    return pl.pallas_call(
        functools.partial(_attn_kernel, tq=tq, meta_mode=meta_mode),
        grid=(batch, nj),
        in_specs=[
            pl.BlockSpec((tq, aw), row_map),
            pl.BlockSpec((tq, 2 * KV_WIDTH), row_map),
            pl.BlockSpec((tq, 2 * KV_WIDTH), row_map),
            pl.BlockSpec((WINDOW, 2 * KV_WIDTH), halo_map),
            pl.BlockSpec((WINDOW, 2 * KV_WIDTH), halo_map),
            pl.BlockSpec((N_META, 2 * KV_WIDTH), const2),
            pl.BlockSpec((N_META, 2 * KV_WIDTH), const2),
            pl.BlockSpec(bias.shape, lambda b, j: (0, 0, 0, 0)),
            pl.BlockSpec(mrow.shape, lambda b, j: (0, 0, 0)),
        ],
        out_specs=pl.BlockSpec((tq, aw), row_map),
        out_shape=jax.ShapeDtypeStruct((batch * seq, aw), BF16),
        scratch_shapes=[
            pltpu.VMEM((2 * WINDOW, 2 * KV_WIDTH), BF16),
            pltpu.VMEM((2 * WINDOW, 2 * KV_WIDTH), BF16),
            pltpu.VMEM((n_hb, N_KV_HEADS, 2 * KCOLS, LANES), BF16),
            pltpu.VMEM((n_hb, N_KV_HEADS, 2 * KCOLS, 2 * LANES), BF16),
        ],
        compiler_params=_cparams(("arbitrary", "arbitrary")),
        name="attn_meta" if meta_mode else "attn",
    )(q, kk, vv, kk, vv, kkm, vvm, bias, mrow)


POOL_HALO = 16


def _pool_inverse_counts(tm, clip_at_row0):
    t = np.arange(tm, dtype=np.float64)[None, :, None]
    w = np.asarray(POOL_WINDOWS, dtype=np.float64)[:, None, None]
    cnt = np.minimum(t + 1, w) if clip_at_row0 else np.broadcast_to(w, (N_POOL_GROUPS, tm, 1))
    return (1.0 / cnt).astype(np.float32)


def _merge_kernel(*refs, with_router, fc):
    (attn_ref, u_ref, uh_ref, um_ref, g_ref, h_ref, wa_ref, wp_ref, ps_ref, wo_ref, pinv_ref,
     gain_ref) = refs[:12]
    if with_router:
        rt_ref, hout_ref, hn_ref, ri_ref, rw_ref = refs[12:]
    else:
        wg_ref, wu_ref, wd_ref, hout_ref = refs[12:]

    j = pl.program_id(1)
    d = h_ref.shape[1]
    a = jnp.dot(attn_ref[...], wa_ref[...], preferred_element_type=F32)

    before = jnp.where(j == 0, um_ref[...], uh_ref[...])
    x = jnp.concatenate([before, u_ref[...]], axis=0).astype(F32)
    parts = []
    for g, w in enumerate(POOL_WINDOWS):
        xg = x[:, g * POOL_GROUP_DIM:(g + 1) * POOL_GROUP_DIM]
        s, k = xg, 1
        while k < w:
            s = s + pltpu.roll(s, k, 0)
            k *= 2
        pooled = s[POOL_HALO:] * pinv_ref[g] - xg[POOL_HALO:]
        parts.append(jnp.dot(pooled.astype(BF16), wp_ref[g], preferred_element_type=F32))
    pb = jnp.concatenate(parts, axis=1) * ps_ref[...]
    gates = g_ref[...]
    merged = gates[:, :d].astype(F32) * a + gates[:, d:].astype(F32) * pb
    out = jnp.dot(merged.astype(BF16), wo_ref[...], preferred_element_type=F32)
    hnew = h_ref[...] + out
    xn = _rms(hnew, gain_ref[...])
    xh = xn.astype(BF16)
    if not with_router:
        hout_ref[...] = _dense_swiglu(xh, hnew, wg_ref, wu_ref, wd_ref, fc)
    else:
        hout_ref[...] = hnew
        hn_ref[...] = xh
        xl = (xn - xh.astype(F32)).astype(BF16)
        nt = (((1,), (1,)), ((), ()))
        t_hi = lax.dot_general(rt_ref[...], xh, nt, preferred_element_type=F32)
        t_lo = lax.dot_general(rt_ref[...], xl, nt, preferred_element_type=F32)
        lg = t_hi[:N_EXPERTS] + t_hi[N_EXPERTS:] + t_lo[:N_EXPERTS]
        eid = lax.broadcasted_iota(jnp.int32, lg.shape, 0).astype(F32)
        m1 = jnp.max(lg, axis=0, keepdims=True)
        i1 = jnp.min(jnp.where(lg == m1, eid, float(N_EXPERTS)), axis=0, keepdims=True)
        lg2 = jnp.where(eid == i1, -jnp.inf, lg)
        m2 = jnp.max(lg2, axis=0, keepdims=True)
        i2 = jnp.min(jnp.where(lg2 == m2, eid, float(N_EXPERTS)), axis=0, keepdims=True)
        e2 = jnp.exp(m2 - m1)
        w1 = 1.0 / (1.0 + e2)
        w2 = e2 / (1.0 + e2)
        row = lax.broadcasted_iota(jnp.int32, lg.shape, 0)
        ri_ref[...] = jnp.where(row == 0, i1, jnp.where(row == 1, i2, 0.0)).astype(jnp.int32)
        rw_ref[...] = jnp.where(row == 0, w1, jnp.where(row == 1, w2, 0.0))


def _merge(attn, u, u_meta, gates, h, wa, wp, pscale, wo, pool_inv, gain, *, rt=None, ffn=None, fc=None,
           batch, seq, tm):
    t, d = h.shape
    nj = seq // tm
    hs = tm // POOL_HALO
    with_router = rt is not None
    assert with_router != (ffn is not None)

    def row_map(b, j):
        return (b * nj + j, 0)

    def halo_map(b, j):
        return (jnp.maximum((b * nj + j) * hs - 1, 0), 0)

    in_specs = [
        pl.BlockSpec((tm, attn.shape[1]), row_map),
        pl.BlockSpec((tm, POOL_WIDTH), row_map),
        pl.BlockSpec((POOL_HALO, POOL_WIDTH), halo_map),
        _resident((POOL_HALO, POOL_WIDTH)),
        pl.BlockSpec((tm, 2 * d), row_map),
        pl.BlockSpec((tm, d), row_map),
        _resident(wa.shape),
        _resident(wp.shape),
        _resident((1, d)),
        _resident(wo.shape),
        _resident(pool_inv.shape),
        _resident((1, d)),
    ]
    args = [attn, u, u, u_meta, gates, h, wa, wp, pscale, wo, pool_inv, gain]
    out_specs = [pl.BlockSpec((tm, d), row_map)]
    out_shape = [jax.ShapeDtypeStruct((t, d), F32)]
    if with_router:
        in_specs.append(_resident(rt.shape))
        args.append(rt)
        lane_map = lambda b, j: (0, b * nj + j)
        out_specs += [pl.BlockSpec((tm, d), row_map), pl.BlockSpec((SUBLANES, tm), lane_map),
                      pl.BlockSpec((SUBLANES, tm), lane_map)]
        out_shape += [jax.ShapeDtypeStruct((t, d), BF16), jax.ShapeDtypeStruct((SUBLANES, t), jnp.int32),
                      jax.ShapeDtypeStruct((SUBLANES, t), F32)]
    else:
        in_specs += [_resident(w.shape) for w in ffn]
        args += list(ffn)
    return pl.pallas_call(
        functools.partial(_merge_kernel, with_router=with_router, fc=fc),
        grid=(batch, nj),
        in_specs=in_specs,
        out_specs=out_specs,
        out_shape=out_shape,
        compiler_params=_cparams(("parallel", "parallel")),
        name="merge_router" if with_router else "merge_ffn",
    )(*args)


def _sigmoid(x):
    return 0.5 * jnp.tanh(0.5 * x) + 0.5


def _swiglu_act(g, u):
    return (g * _sigmoid(g)) * u


def _dense_swiglu(x, residual, wg_ref, wu_ref, wd_ref, fc):
    acc = residual
    for c in range(wg_ref.shape[1] // fc):
        cols = slice(c * fc, (c + 1) * fc)
        g = jnp.dot(x, wg_ref[:, cols], preferred_element_type=F32)
        u = jnp.dot(x, wu_ref[:, cols], preferred_element_type=F32)
        acc = acc + jnp.dot(_swiglu_act(g, u).astype(BF16), wd_ref[cols, :], preferred_element_type=F32)
    return acc


E_ROWS = 16


def _col_to_lanes(col, width):
    r = lax.broadcasted_iota(jnp.int32, (E_ROWS, width), 0)
    c = lax.broadcasted_iota(jnp.int32, (E_ROWS, width), 1)
    return jnp.sum(jnp.where(r == c, col, 0.0), axis=0, keepdims=True)


def _excl_cumsum_col(col):
    r = lax.broadcasted_iota(jnp.int32, (E_ROWS, E_ROWS), 0)
    c = lax.broadcasted_iota(jnp.int32, (E_ROWS, E_ROWS), 1)
    return jnp.sum(jnp.where(c < r, _col_to_lanes(col, E_ROWS), 0.0), axis=1, keepdims=True)


def _route_tab_kernel(ri_ref, rw_ref, tri_ref, slot_ref, tab_ref, fin_ref, carry, *, tn, tmg, nt_pad):
    i = pl.program_id(0)

    @pl.when(i == 0)
    def _():
        carry[...] = jnp.zeros_like(carry)

    eid = lax.broadcasted_iota(jnp.int32, (E_ROWS, tn), 0)
    ri = ri_ref[...]
    oh0 = (eid == ri[0:1, :]).astype(F32)
    oh1 = (eid == ri[1:2, :]).astype(F32)
    oh = oh0 + oh1
    cum = jnp.dot(oh.astype(BF16), tri_ref[...], preferred_element_type=F32)
    cnt = jnp.sum(oh, axis=1, keepdims=True)
    base = _excl_cumsum_col(cnt) + cum
    s0 = jnp.sum(oh0 * base, axis=0, keepdims=True)
    s1 = jnp.sum(oh1 * base, axis=0, keepdims=True)
    row = lax.broadcasted_iota(jnp.int32, (SUBLANES, tn), 0)
    rw = rw_ref[...]
    slot_ref[...] = jnp.where(row == 0, s0, jnp.where(row == 1, s1, jnp.where(row == 2, rw[0:1, :],
                              jnp.where(row == 3, rw[1:2, :], 0.0))))
    trow = lax.broadcasted_iota(jnp.int32, (SUBLANES, LANES), 0)
    tab_ref[...] = jnp.where(trow == 0, _col_to_lanes(cnt, LANES),
                             jnp.where(trow == 1, _col_to_lanes(carry[...], LANES), 0.0)).astype(jnp.int32)
    carry[...] += cnt

    @pl.when(i == pl.num_programs(0) - 1)
    def _():
        total = carry[...]
        padded = jnp.ceil(total / tmg) * tmg
        st = _excl_cumsum_col(padded)
        ends = st + padded
        tile0 = (lax.broadcasted_iota(jnp.int32, (E_ROWS, nt_pad), 1) * tmg).astype(F32)
        texp = jnp.minimum(jnp.sum((tile0 >= ends).astype(F32), axis=0, keepdims=True), N_EXPERTS - 1.0)
        nused = jnp.max(ends, axis=0, keepdims=True) / tmg
        frow = lax.broadcasted_iota(jnp.int32, (SUBLANES, nt_pad), 0)
        fin_ref[...] = jnp.where(frow == 0, texp, jnp.where(frow == 1, nused, jnp.where(
            frow == 2, _col_to_lanes(st, nt_pad), jnp.where(frow == 3, _col_to_lanes(total, nt_pad), 0.0)))
        ).astype(jnp.int32)


def _route_tables(ri, rw, tn, tmg, nt_pad):
    t = ri.shape[1]
    nb = t // tn
    tri = jnp.asarray(np.triu(np.ones((tn, tn), dtype=np.float32), k=1), dtype=BF16)
    lane_blk = pl.BlockSpec((SUBLANES, tn), lambda i: (0, i))
    return pl.pallas_call(
        functools.partial(_route_tab_kernel, tn=tn, tmg=tmg, nt_pad=nt_pad),
        grid=(nb,),
        in_specs=[lane_blk, lane_blk, pl.BlockSpec((tn, tn), lambda i: (0, 0))],
        out_specs=[lane_blk, pl.BlockSpec((None, SUBLANES, LANES), lambda i: (i, 0, 0)),
                   pl.BlockSpec((SUBLANES, nt_pad), lambda i: (0, 0))],
        out_shape=[jax.ShapeDtypeStruct((SUBLANES, t), F32), jax.ShapeDtypeStruct((nb, SUBLANES, LANES), jnp.int32),
                   jax.ShapeDtypeStruct((SUBLANES, nt_pad), jnp.int32)],
        scratch_shapes=[pltpu.VMEM((E_ROWS, 1), F32)],
        compiler_params=_cparams(("arbitrary",)),
        name="route_tab",
    )(ri, rw, tri)


ROW_CHUNKS = 8
ROW_PITCH = 9


def _pieces(n, max_log2, fn):
    for b in reversed(range(max_log2 + 1)):
        above = (n >> (b + 1)) << (b + 1)

        @pl.when(((n >> b) & 1) == 1)
        def _():
            fn(above, 1 << b)


def _tile_runs(i, ntab, basetab, gstart, tm, fn):
    off = jnp.int32(0)
    for e in range(N_EXPERTS):
        n_e = ntab[i * E_ROWS + e]
        dst = gstart[e] + basetab[i * E_ROWS + e]
        _pieces(n_e, tm.bit_length() - 1, lambda o, size, off=off, dst=dst: fn(off + o, dst + o, size))
        off = off + n_e


def _rows(ref, start, size):
    return ref.at[pl.ds(start * ROW_PITCH, size * ROW_PITCH)]


def _scatter_kernel(ntab, basetab, gstart, gcount, x_ref, slot_ref, xs_hbm, sbuf, zbuf, sem, zsem, *, tm, tmg):
    i = pl.program_id(0)
    last = pl.num_programs(0) - 1
    n_sorted = xs_hbm.shape[0] // ROW_PITCH
    half = lax.rem(i, 2)
    slots = slot_ref[...]
    rowid = lax.broadcasted_iota(jnp.int32, (2 * tm, tm), 0).astype(F32)
    perm = jnp.where(jnp.logical_or(rowid == slots[0:1, :], rowid == slots[1:2, :]), 1.0, 0.0).astype(BF16)
    srt = jnp.dot(perm, x_ref[...], preferred_element_type=F32)
    @pl.when(i == 0)
    def _():
        sbuf[...] = jnp.zeros_like(sbuf)

    sub0 = half * (2 * tm * ROW_PITCH)
    for c in range(ROW_CHUNKS):
        sbuf[pl.ds(sub0 + c, 2 * tm, stride=ROW_PITCH), :] = srt[:, c * LANES:(c + 1) * LANES]

    def runs(tile, which, act):
        def go(src, dst, size):
            cp = pltpu.make_async_copy(_rows(sbuf, which * (2 * tm) + src, size), _rows(xs_hbm, dst, size),
                                       sem.at[which])
            getattr(cp, act)()
        _tile_runs(tile, ntab, basetab, gstart, tm, go)

    @pl.when(i > 0)
    def _():
        runs(i - 1, 1 - half, "wait")

    runs(i, half, "start")

    @pl.when(i == last)
    def _():
        runs(i, half, "wait")
        zbuf[...] = jnp.zeros_like(zbuf)
        zrows = zbuf.shape[0] // ROW_PITCH

        def zcopy(dst, size):
            return pltpu.make_async_copy(_rows(zbuf, 0, size), _rows(xs_hbm, dst, size), zsem)

        used = gstart[N_EXPERTS]
        for act in ("start", "wait"):
            for e in range(N_EXPERTS):
                end = gstart[e] + gcount[e]
                npad = (tmg - (gcount[e] & (tmg - 1))) & (tmg - 1)
                _pieces(npad, zrows.bit_length() - 1,
                        lambda o, size, end=end, act=act: getattr(zcopy(end + o, size), act)())
            for k in range(N_EXPERTS * tmg // zrows):
                @pl.when(used + k * zrows < n_sorted)
                def _():
                    getattr(zcopy(used + k * zrows, zrows), act)()


def _scatter_rows(ntab, basetab, gstart, gcount, x, slotw, n_sorted, tm, tmg):
    t, d = x.shape
    assert d == ROW_CHUNKS * LANES and tmg & (tmg - 1) == 0 and tm & (tm - 1) == 0
    grid_spec = pltpu.PrefetchScalarGridSpec(
        num_scalar_prefetch=4,
        grid=(t // tm,),
        in_specs=[pl.BlockSpec((tm, d), lambda i, *_: (i, 0)), pl.BlockSpec((SUBLANES, tm), lambda i, *_: (0, i))],
        out_specs=pl.BlockSpec(memory_space=pl.ANY),
        scratch_shapes=[pltpu.VMEM((2 * 2 * tm * ROW_PITCH, LANES), F32),
                        pltpu.VMEM((tmg // 2 * ROW_PITCH, LANES), F32),
                        pltpu.SemaphoreType.DMA((2,)), pltpu.SemaphoreType.DMA],
    )
    return pl.pallas_call(
        functools.partial(_scatter_kernel, tm=tm, tmg=tmg),
        grid_spec=grid_spec,
        out_shape=jax.ShapeDtypeStruct((n_sorted * ROW_PITCH, LANES), F32),
        compiler_params=_cparams(("arbitrary",)),
        name="moe_scatter",
    )(ntab, basetab, gstart, gcount, x, slotw)


def _grouped_ffn_kernel(te_ref, nu_ref, x_ref, wg_ref, wu_ref, wd_ref, y_ref, xb, acc):
    i = pl.program_id(0)
    j = pl.program_id(1)
    nj = pl.num_programs(1)

    @pl.when(i < nu_ref[0])
    def _():
        tmg = xb.shape[0]

        @pl.when(j == 0)
        def _():
            for c in range(ROW_CHUNKS):
                xb[:, c * LANES:(c + 1) * LANES] = x_ref[pl.ds(c, tmg, stride=ROW_PITCH), :].astype(BF16)
            acc[...] = jnp.zeros_like(acc)

        x = xb[...]
        g = jnp.dot(x, wg_ref[...], preferred_element_type=F32)
        u = jnp.dot(x, wu_ref[...], preferred_element_type=F32)
        acc[...] += jnp.dot(_swiglu_act(g, u).astype(BF16), wd_ref[...], preferred_element_type=F32)

        @pl.when(j == nj - 1)
        def _():
            for c in range(ROW_CHUNKS):
                y_ref[pl.ds(c, tmg, stride=ROW_PITCH), :] = acc[:, c * LANES:(c + 1) * LANES]
            for c in range(ROW_CHUNKS, ROW_PITCH):
                y_ref[pl.ds(c, tmg, stride=ROW_PITCH), :] = jnp.zeros((tmg, LANES), F32)

    @pl.when(jnp.logical_and(i >= nu_ref[0], j == nj - 1))
    def _():
        y_ref[...] = jnp.zeros_like(y_ref)


def _grouped_ffn(te, nused, xs, wg, wu, wd, tmg, fc, nt):
    d, f = wg.shape[1], wg.shape[2]
    nj = f // fc
    blk = tmg * ROW_PITCH

    def tile(i, nu):
        return jnp.minimum(i, nu[0] - 1)

    def chunk(i, j, nu):
        return jnp.where(i < nu[0], j, nj - 1)

    grid_spec = pltpu.PrefetchScalarGridSpec(
        num_scalar_prefetch=2,
        grid=(nt, nj),
        in_specs=[
            pl.BlockSpec((blk, LANES), lambda i, j, te, nu: (tile(i, nu), 0)),
            pl.BlockSpec((None, d, fc), lambda i, j, te, nu: (te[tile(i, nu)], 0, chunk(i, j, nu))),
            pl.BlockSpec((None, d, fc), lambda i, j, te, nu: (te[tile(i, nu)], 0, chunk(i, j, nu))),
            pl.BlockSpec((None, fc, d), lambda i, j, te, nu: (te[tile(i, nu)], chunk(i, j, nu), 0)),
        ],
        out_specs=pl.BlockSpec((blk, LANES), lambda i, j, te, nu: (i, 0)),
        scratch_shapes=[pltpu.VMEM((tmg, d), BF16), pltpu.VMEM((tmg, d), F32)],
    )
    return pl.pallas_call(
        _grouped_ffn_kernel,
        grid_spec=grid_spec,
        out_shape=jax.ShapeDtypeStruct(xs.shape, F32),
        compiler_params=_cparams(("arbitrary", "arbitrary")),
        name="moe_ffn",
    )(te, nused, xs, wg, wu, wd)


def _combine_kernel(ntab, basetab, gstart, ys_hbm, slot_ref, h_ref, gain_ref, o_ref, ybuf, yg, sem, *, tm):
    i = pl.program_id(0)
    half = lax.rem(i, 2)

    def runs(tile, which, act):
        def go(loc, src, size):
            cp = pltpu.make_async_copy(_rows(ys_hbm, src, size), _rows(ybuf, which * (2 * tm) + loc, size),
                                       sem.at[which])
            getattr(cp, act)()
        _tile_runs(tile, ntab, basetab, gstart, tm, go)

    @pl.when(i == 0)
    def _():
        runs(i, half, "start")

    @pl.when(i + 1 < pl.num_programs(0))
    def _():
        runs(i + 1, 1 - half, "start")

    runs(i, half, "wait")
    sub0 = half * (2 * tm * ROW_PITCH)
    for c in range(ROW_CHUNKS):
        yg[:, c * LANES:(c + 1) * LANES] = ybuf[pl.ds(sub0 + c, 2 * tm, stride=ROW_PITCH), :].astype(BF16)

    sw = jnp.concatenate([slot_ref[...], jnp.zeros((LANES - SUBLANES, tm), F32)], axis=0).T
    lane = lax.broadcasted_iota(jnp.int32, (tm, 2 * tm), 1).astype(F32)
    pick = jnp.where(lane == sw[:, 0:1], sw[:, 2:3], jnp.where(lane == sw[:, 1:2], sw[:, 3:4], 0.0))
    y = h_ref[...] + jnp.dot(pick.astype(BF16), yg[...], preferred_element_type=F32)
    o_ref[...] = _rms(y, gain_ref[...])


def _combine(ntab, basetab, gstart, ys, slotw, h, gain, tm):
    t, d = h.shape
    grid_spec = pltpu.PrefetchScalarGridSpec(
        num_scalar_prefetch=3,
        grid=(t // tm,),
        in_specs=[pl.BlockSpec(memory_space=pl.ANY),
                  pl.BlockSpec((SUBLANES, tm), lambda i, *_: (0, i)),
                  pl.BlockSpec((tm, d), lambda i, *_: (i, 0)),
                  pl.BlockSpec((1, d), lambda i, *_: (0, 0))],
        out_specs=pl.BlockSpec((tm, d), lambda i, *_: (i, 0)),
        scratch_shapes=[pltpu.VMEM((2 * 2 * tm * ROW_PITCH, LANES), F32), pltpu.VMEM((2 * tm, d), BF16),
                        pltpu.SemaphoreType.DMA((2,))],
    )
    return pl.pallas_call(
        functools.partial(_combine_kernel, tm=tm),
        grid_spec=grid_spec,
        out_shape=jax.ShapeDtypeStruct((t, d), F32),
        compiler_params=_cparams(("arbitrary",)),
        name="moe_combine",
    )(ntab, basetab, gstart, ys, slotw, h, gain)


def _half_if_aligned(n):
    return n // 2 if n % (2 * LANES) == 0 else n


def _tiles(seq, d_ff, d_ff_expert):
    return dict(
        tm_in=1024,
        tq=min(1024, seq),
        tm_mg=512,
        fc_ffn=d_ff,
        tm_sc=512,
        tmg=512, fc_moe=_half_if_aligned(d_ff_expert),
    )


def kernel(x, meta_tokens, norm_mix, w_in, attn_sinks, w_attn_br, w_pool_grp, pool_scale, w_out, norm_ffn,
           dense_w_gate, dense_w_up, dense_w_down, moe_router, moe_w_gate, moe_w_up, moe_w_down, norm_final):
    batch, seq, d = x.shape
    depth = w_in.shape[0]
    assert depth == 2 and dense_w_gate.shape[0] == 1 and moe_router.shape[0] == 1, "dense layer then expert layer"
    assert moe_router.shape[2] == N_EXPERTS and w_in.shape[2] == N_HEADS * HEAD_DIM + 2 * KV_WIDTH + POOL_WIDTH + 2 * d
    t = batch * seq
    cfg = _tiles(seq, dense_w_gate.shape[2], moe_w_gate.shape[3])
    bf = lambda a: a.astype(BF16)

    bias_np, mrow_np, sinkpos_np = _attn_bias_tables()
    mrow = jnp.asarray(mrow_np)
    pool_inv_main = jnp.asarray(_pool_inverse_counts(cfg["tm_mg"], False))
    pool_inv_meta = jnp.asarray(_pool_inverse_counts(META_ROWS, True))

    h = x.reshape(t, d)
    hm = jnp.concatenate([meta_tokens.astype(F32), jnp.zeros((META_ROWS - N_META, d), F32)], axis=0)

    out = None
    for layer in range(depth):
        gain = norm_mix[layer].reshape(1, d)
        w_in_bf = bf(w_in[layer])
        sink_row = jnp.repeat(attn_sinks[layer].astype(F32) * LOG2E, KCOLS).reshape(N_HEADS // 2, 1, 2 * KCOLS)
        bias = jnp.asarray(bias_np) + (jnp.asarray(sinkpos_np) * sink_row)[None]
        wa, wo = bf(w_attn_br[layer]), bf(w_out[layer])
        wp = bf(w_pool_grp[layer])
        ps = pool_scale[layer].reshape(1, d)
        gain_ffn = norm_ffn[layer].reshape(1, d)

        qm, kkm_all, vvm_all, um, gm = _inproj(hm, gain, w_in_bf, META_ROWS)
        kkm, vvm = kkm_all[:N_META], vvm_all[:N_META]
        um16 = um[:N_META]
        q, kk, vv, u, g = _inproj(h, gain, w_in_bf, cfg["tm_in"])
        attn = _attention(q, kk, vv, kkm, vvm, bias, mrow, batch=batch, seq=seq, tq=cfg["tq"], meta_mode=False)

        u_before = um16
        if layer == 0:
            attn_m = _attention(qm, kkm_all, vvm_all, kkm, vvm, bias, mrow,
                                batch=1, seq=META_ROWS, tq=META_ROWS, meta_mode=True)
            ffn = (bf(dense_w_gate[0]), bf(dense_w_up[0]), bf(dense_w_down[0]))
            hm, = _merge(attn_m, um, jnp.zeros_like(u_before), gm, hm, wa, wp, ps, wo, pool_inv_meta, gain_ffn,
                         ffn=ffn, fc=cfg["fc_ffn"], batch=1, seq=META_ROWS, tm=META_ROWS)
            h, = _merge(attn, u, u_before, g, h, wa, wp, ps, wo, pool_inv_main, gain_ffn,
                        ffn=ffn, fc=cfg["fc_ffn"], batch=batch, seq=seq, tm=cfg["tm_mg"])
        else:
            r = moe_router[0].astype(F32)
            r_hi = r.astype(BF16)
            r_lo = (r - r_hi.astype(F32)).astype(BF16)
            rt = jnp.concatenate([r_hi.T, r_lo.T], axis=0)
            h, hn, ri, rw = _merge(attn, u, u_before, g, h, wa, wp, ps, wo, pool_inv_main, gain_ffn, rt=rt,
                                   batch=batch, seq=seq, tm=cfg["tm_mg"])
            wg, wu, wd = bf(moe_w_gate[0]), bf(moe_w_up[0]), bf(moe_w_down[0])
            out = _moe(h, hn, ri, rw, wg, wu, wd, norm_final.reshape(1, d), cfg)
    return out.reshape(batch, seq, d)


def _moe(h, hn, ri, rw, wg, wu, wd, gain_final, cfg):
    t, d = h.shape
    tmg, tm = cfg["tmg"], cfg["tm_sc"]
    nt = 2 * t // tmg + N_EXPERTS
    nt_pad = -(-nt // LANES) * LANES
    slotw, tab, fin = _route_tables(ri, rw, tm, tmg, nt_pad)
    ntab = tab[:, 0, :E_ROWS].reshape(-1)
    basetab = tab[:, 1, :E_ROWS].reshape(-1)
    te, nused, gstart, gcount = fin[0, :nt], fin[1, :1], fin[2, :E_ROWS], fin[3, :E_ROWS]
    xs = _scatter_rows(ntab, basetab, gstart, gcount, hn, slotw, nt * tmg, tm, tmg)
    ys = _grouped_ffn(te, nused, xs, wg, wu, wd, tmg, cfg["fc_moe"], nt)
    return _combine(ntab, basetab, gstart, ys, slotw, h, gain_final, tm)
```
